```python
import math
import jax
import jax.numpy as jnp
from jax import lax
import numpy as np

D_MODEL = 1024
BATCH = 8
SEQ = 2048
DEPTH = 2
DEC_BATCH = 32
DEC_SEQ = 1
PAST_LEN = 16384
PAGE_SIZE = 128

HEAD_DIM = 64
SB_HEADS = D_MODEL // HEAD_DIM
NSA_HEADS = D_MODEL // HEAD_DIM
NSA_GROUPS = 4
NSA_REP = NSA_HEADS // NSA_GROUPS
NSA_KV_WIDTH = NSA_GROUPS * HEAD_DIM
CMP_BLOCK = 32
SEL_BLOCK = 64
N_SEL = 16
WINDOW = 512
Q_BLOCK = 128
D_FF = 4 * D_MODEL
N_BUCKETS = 32
MAX_DISTANCE = 128
N_SB_LAYERS = (DEPTH + 1) // 2
N_NSA_LAYERS = DEPTH // 2
DEEPNORM_ALPHA = (2 * DEPTH) ** 0.25
DEEPNORM_BETA = (8 * DEPTH) ** -0.25
LN_EPS = 1e-5
SEL_FORCE = 1e4

kernel_name = 'hybrid_stickbreak_nsa_decode_step'


def _layer_norm(x, g, b):
    xf = x.astype(jnp.float32)
    xc = xf - xf.mean(-1, keepdims=True)
    var = (xc * xc).mean(-1, keepdims=True)
    return (xc * lax.rsqrt(var + LN_EPS) * g.astype(jnp.float32) + b.astype(jnp.float32)).astype(x.dtype)


def _sq_relu_mlp(x, w_up, w_down):
    h = jax.nn.relu(x @ w_up)
    return (h * h) @ w_down


def _rel_bucket(dist):
    dist = jnp.maximum(dist, 0)
    max_exact = N_BUCKETS // 2
    d = jnp.maximum(dist, 1).astype(jnp.float32)
    large = max_exact + (jnp.log(d / max_exact) / math.log(MAX_DISTANCE / max_exact)
                         * (N_BUCKETS - max_exact)).astype(jnp.int32)
    large = jnp.minimum(large, N_BUCKETS - 1)
    return jnp.where(dist < max_exact, dist, large)


def _head_bias(rel_bias, dist):
    b = rel_bias[_rel_bucket(dist)].astype(jnp.float32)
    return jnp.moveaxis(b, -1, 0).reshape(NSA_GROUPS, NSA_REP, *dist.shape)


def _masked_softmax(z, valid):
    z = jnp.where(valid, z, -jnp.inf)
    m = jnp.max(z, axis=-1, keepdims=True)
    m = jnp.where(jnp.isfinite(m), m, 0.0)
    e = jnp.where(valid, jnp.exp(z - m), 0.0)
    return e / jnp.maximum(e.sum(-1, keepdims=True), 1e-30)


def _sb_chunk(q, k, v, valid, suffix):
    z = jnp.einsum('bqhd,bshd->bhqs', q, k, preferred_element_type=jnp.float32) * HEAD_DIM ** -0.5
    sp = jnp.where(valid, jax.nn.softplus(z), 0.0)
    right = lax.cumsum(sp, axis=3, reverse=True) - sp + suffix[..., None]
    a = jnp.where(valid, jnp.exp(jax.nn.log_sigmoid(z) - right), 0.0)
    o = jnp.einsum('bhqs,bshd->bqhd', a.astype(v.dtype), v, preferred_element_type=jnp.float32)
    return o, suffix + sp.sum(-1)


def _sb_prompt(q, k, v):
    B, T = q.shape[:2]
    nqb = T // Q_BLOCK
    qb = q.reshape(B, nqb, Q_BLOCK, SB_HEADS, HEAD_DIM).swapaxes(0, 1)
    kpos = jnp.arange(T)

    def block(args):
        qi, i = args
        qpos = i * Q_BLOCK + jnp.arange(Q_BLOCK)
        o, _ = _sb_chunk(qi, k, v, kpos[None, :] < qpos[:, None],
                         jnp.zeros((B, SB_HEADS, Q_BLOCK), jnp.float32))
        return o

    o = lax.map(block, (qb, jnp.arange(nqb)))
    return o.swapaxes(0, 1).reshape(B, T, SB_HEADS * HEAD_DIM)


def _sb_sample(q, k, v, cache_kv, layer, page_table):
    B, Tn = q.shape[:2]
    tri = jnp.arange(Tn)[None, :] < jnp.arange(Tn)[:, None]
    o, suffix = _sb_chunk(q, k, v, tri, jnp.zeros((B, SB_HEADS, Tn), jnp.float32))

    def page_step(carry, page_ids):
        acc, suf = carry
        kv = cache_kv[layer, page_ids]
        o_p, suf = _sb_chunk(q, kv[:, :, 0], kv[:, :, 1], True, suf)
        return (acc + o_p, suf), None

    (o, _), _ = lax.scan(page_step, (o, suffix), page_table.T[::-1])
    return o.reshape(B, Tn, SB_HEADS * HEAD_DIM)


def _nsa_split(h):
    B, T = h.shape[:2]
    qw = NSA_HEADS * HEAD_DIM
    q = h[..., :qw].reshape(B, T, NSA_GROUPS, NSA_REP, HEAD_DIM)
    kv = h[..., qw:qw + 6 * NSA_KV_WIDTH].reshape(B, T, 3, 2, NSA_GROUPS, HEAD_DIM)
    gates = jax.nn.sigmoid(h[..., qw + 6 * NSA_KV_WIDTH:].astype(jnp.float32)).reshape(B, T, 3, NSA_GROUPS, NSA_REP)
    return q, kv[:, :, 0], kv[:, :, 1], kv[:, :, 2], gates


def _compress(kv, pos, w):
    B, T = kv.shape[:2]
    xb = kv.reshape(B, T // CMP_BLOCK, CMP_BLOCK, 2, NSA_GROUPS, HEAD_DIM) + jnp.swapaxes(pos, 0, 1)[:, :, None, :]
    return jnp.einsum('bnlcgd,clde->bncge', xb, w)


def _nsa_attend(q, qpos, kvc, gather_sel, n_sel_blocks, kvw, kwpos, gates, rel_bias):
    B, Tq = q.shape[:2]
    scale = HEAD_DIM ** -0.5
    nc = kvc.shape[1]
    dist_c = qpos[:, None] - (jnp.arange(nc) * CMP_BLOCK + CMP_BLOCK - 1)[None, :]
    z_c = jnp.einsum('bqgrd,bngd->bgrqn', q, kvc[:, :, 0], preferred_element_type=jnp.float32) * scale \
        + _head_bias(rel_bias, dist_c)
    p_c = _masked_softmax(z_c, dist_c >= 0)
    o_c = jnp.einsum('bgrqn,bngd->bqgrd', p_c.astype(kvc.dtype), kvc[:, :, 1], preferred_element_type=jnp.float32)
    ratio = SEL_BLOCK // CMP_BLOCK
    imp = jnp.pad(p_c.sum(2), ((0, 0), (0, 0), (0, 0), (0, n_sel_blocks * ratio - nc)))
    imp = imp.reshape(B, NSA_GROUPS, Tq, n_sel_blocks, ratio).sum(-1)
    blk = jnp.arange(n_sel_blocks)[None, :]
    cur = (qpos // SEL_BLOCK)[:, None]
    forced = (blk == 0) | (blk == cur) | (blk == cur - 1)
    score = jnp.where(blk * SEL_BLOCK > qpos[:, None], -jnp.inf, jnp.where(forced, SEL_FORCE, imp))
    _, idx = lax.top_k(score, min(N_SEL, n_sel_blocks))
    kvs = gather_sel(idx)
    kpos = idx[..., None] * SEL_BLOCK + jnp.arange(SEL_BLOCK)
    dist_s = qpos[:, None, None] - kpos
    g_ix = jnp.arange(NSA_GROUPS)[None, :, None, None, None]
    b_s = rel_bias.reshape(N_BUCKETS, NSA_GROUPS, NSA_REP)[_rel_bucket(dist_s), g_ix]
    z_s = jnp.einsum('bqgrd,bgqjsd->bgrqjs', q, kvs[..., 0, :], preferred_element_type=jnp.float32) * scale \
        + jnp.moveaxis(b_s, -1, 2).astype(jnp.float32)
    p_s = _masked_softmax(z_s.reshape(B, NSA_GROUPS, NSA_REP, Tq, -1),
                          (dist_s >= 0).reshape(B, NSA_GROUPS, 1, Tq, -1))
    o_s = jnp.einsum('bgrqjs,bgqjsd->bqgrd', p_s.reshape(z_s.shape).astype(kvs.dtype), kvs[..., 1, :],
                     preferred_element_type=jnp.float32)
    dist_w = qpos[:, None] - kwpos[None, :]
    valid_w = (dist_w >= 0) & (dist_w <= WINDOW) & (kwpos >= 0)[None, :]
    z_w = jnp.einsum('bqgrd,bsgd->bgrqs', q, kvw[:, :, 0], preferred_element_type=jnp.float32) * scale \
        + _head_bias(rel_bias, dist_w)
    p_w = _masked_softmax(z_w, valid_w)
    o_w = jnp.einsum('bgrqs,bsgd->bqgrd', p_w.astype(kvw.dtype), kvw[:, :, 1], preferred_element_type=jnp.float32)
    o = gates[:, :, 0, ..., None] * o_c + gates[:, :, 1, ..., None] * o_s + gates[:, :, 2, ..., None] * o_w
    return o.reshape(B, Tq, NSA_HEADS * HEAD_DIM)


def _nsa_prompt(h, pos_c, w_c, rel_bias):
    B, T = h.shape[:2]
    q, kv_c, kv_s, kv_w, gates = _nsa_split(h)
    kvc = _compress(kv_c, pos_c, w_c)
    n_sel_blocks = T // SEL_BLOCK
    kvs_blocks = kv_s.reshape(B, n_sel_blocks, SEL_BLOCK, 2, NSA_GROUPS, HEAD_DIM)
    b_ix = jnp.arange(B)[:, None, None, None]
    g_ix = jnp.arange(NSA_GROUPS)[None, :, None, None]

    def gather_sel(idx):
        return kvs_blocks[b_ix, idx, :, :, g_ix, :]

    kvw_pad = jnp.pad(kv_w, ((0, 0), (WINDOW, 0), (0, 0), (0, 0), (0, 0)))
    nqb = T // Q_BLOCK
    qb = q.reshape(B, nqb, Q_BLOCK, NSA_GROUPS, NSA_REP, HEAD_DIM).swapaxes(0, 1)
    gb = gates.reshape(B, nqb, Q_BLOCK, 3, NSA_GROUPS, NSA_REP).swapaxes(0, 1)

    def block(args):
        qi, gi, i = args
        start = i * Q_BLOCK
        qpos = start + jnp.arange(Q_BLOCK)
        kvw = lax.dynamic_slice_in_dim(kvw_pad, start, WINDOW + Q_BLOCK, axis=1)
        kwpos = start - WINDOW + jnp.arange(WINDOW + Q_BLOCK)
        return _nsa_attend(qi, qpos, kvc, gather_sel, n_sel_blocks, kvw, kwpos, gi, rel_bias)

    o = lax.map(block, (qb, gb, jnp.arange(nqb)))
    o = o.swapaxes(0, 1).reshape(B, T, NSA_HEADS * HEAD_DIM)
    w_buf = min(WINDOW, T)
    return o, kv_c, kv_s, kv_w[:, T - w_buf:]


def _nsa_sample(h, cache_cmp_kv, cache_sel_kv, win_kv, layer, page_table, pos_c, w_c, rel_bias):
    B, Tn = h.shape[:2]
    past = page_table.shape[1] * PAGE_SIZE
    total = past + Tn
    q, kv_c, kv_s, kv_w, gates = _nsa_split(h)
    qpos = past + jnp.arange(Tn)
    past_c = cache_cmp_kv[layer, page_table].reshape(B, past, 2, NSA_GROUPS, HEAD_DIM)
    n_cmp = total // CMP_BLOCK
    kvc = _compress(jnp.concatenate([past_c, kv_c], axis=1)[:, :n_cmp * CMP_BLOCK], pos_c, w_c)
    n_sel_blocks = -(-total // SEL_BLOCK)
    n_past_blocks = past // SEL_BLOCK
    n_new_blocks = n_sel_blocks - n_past_blocks
    per_page = PAGE_SIZE // SEL_BLOCK
    new_blocks = jnp.pad(kv_s, ((0, 0), (0, n_new_blocks * SEL_BLOCK - Tn), (0, 0), (0, 0), (0, 0)))
    new_blocks = new_blocks.reshape(B, n_new_blocks, SEL_BLOCK, 2, NSA_GROUPS, HEAD_DIM)
    b_ix = jnp.arange(B)[:, None, None, None]
    g_ix = jnp.arange(NSA_GROUPS)[None, :, None, None]

    def gather_sel(idx):
        pidx = jnp.minimum(idx, n_past_blocks - 1)
        phys = page_table[b_ix, pidx // per_page]
        rows = (pidx % per_page)[..., None] * SEL_BLOCK + jnp.arange(SEL_BLOCK)
        from_past = cache_sel_kv[layer, phys[..., None], rows, :, g_ix[..., None], :]
        from_new = new_blocks[b_ix, jnp.clip(idx - n_past_blocks, 0, n_new_blocks - 1), :, :, g_ix, :]
        return jnp.where((idx >= n_past_blocks)[..., None, None, None], from_new, from_past)

    w_buf = win_kv.shape[2]
    kvw = jnp.concatenate([win_kv[layer], kv_w], axis=1)
    kwpos = past - w_buf + jnp.arange(w_buf + Tn)
    o = _nsa_attend(q, qpos, kvc, gather_sel, n_sel_blocks, kvw, kwpos, gates, rel_bias)
    return o, kv_c, kv_s, kvw[:, Tn:]


def setup_inputs(seed: int = 0) -> dict:
    key = jax.random.key(seed)
    ks = jax.random.split(key, 20)
    nk = jax.random.split(ks[19], 8)
    n_pages = PAST_LEN // PAGE_SIZE
    n_pool = (5 * DEC_BATCH * n_pages + 3) // 4
    w_buf = min(WINDOW, PAST_LEN)
    fan = D_MODEL ** -0.5
    beta = DEEPNORM_BETA

    def nrm(k, shape, scale=1.0):
        return jax.random.normal(k, shape, jnp.float32) * scale

    x_prompt = nrm(ks[0], (BATCH, SEQ, D_MODEL))
    x_sample = nrm(ks[1], (DEC_BATCH, DEC_SEQ, D_MODEL))
    cache_sb_kv = nrm(ks[2], (N_SB_LAYERS, n_pool, PAGE_SIZE, 2, SB_HEADS, HEAD_DIM))
    cache_cmp_kv = nrm(ks[3], (N_NSA_LAYERS, n_pool, PAGE_SIZE, 2, NSA_GROUPS, HEAD_DIM))
    cache_sel_kv = nrm(ks[4], (N_NSA_LAYERS, n_pool, PAGE_SIZE, 2, NSA_GROUPS, HEAD_DIM))
    state_win_kv = nrm(ks[5], (N_NSA_LAYERS, DEC_BATCH, w_buf, 2, NSA_GROUPS, HEAD_DIM))
    page_table = jax.random.permutation(ks[6], n_pool)[:DEC_BATCH * n_pages].reshape(DEC_BATCH, n_pages).astype(jnp.int32)
    w_in_sb = jnp.concatenate([nrm(ks[7], (N_SB_LAYERS, D_MODEL, 2 * D_MODEL), fan),
                               nrm(ks[8], (N_SB_LAYERS, D_MODEL, D_MODEL), fan * beta)], axis=-1)
    w_out_sb = nrm(ks[9], (N_SB_LAYERS, D_MODEL, D_MODEL), fan * beta)
    kvs = (N_NSA_LAYERS, D_MODEL, NSA_KV_WIDTH)
    w_in_nsa = jnp.concatenate([
        nrm(nk[0], (N_NSA_LAYERS, D_MODEL, NSA_HEADS * HEAD_DIM), fan),
        nrm(nk[1], kvs, fan), nrm(nk[2], kvs, fan * beta),
        nrm(nk[3], kvs, fan), nrm(nk[4], kvs, fan * beta),
        nrm(nk[5], kvs, fan), nrm(nk[6], kvs, fan * beta),
        nrm(nk[7], (N_NSA_LAYERS, D_MODEL, 3 * NSA_HEADS), fan)], axis=-1)
    w_out_nsa = nrm(ks[10], (N_NSA_LAYERS, D_MODEL, D_MODEL), fan * beta)
    cmp_pos = nrm(ks[11], (N_NSA_LAYERS, 2, CMP_BLOCK, HEAD_DIM), 0.5)
    w_cmp = nrm(ks[12], (N_NSA_LAYERS, 2, CMP_BLOCK, HEAD_DIM, HEAD_DIM), (CMP_BLOCK * HEAD_DIM) ** -0.5)
    rel_bias = nrm(ks[13], (N_BUCKETS, NSA_HEADS), 0.5)
    ln_g = 1.0 + nrm(ks[14], (DEPTH, 2, D_MODEL), 0.02)
    ln_b = nrm(ks[15], (DEPTH, 2, D_MODEL), 0.02)
    w_up = nrm(ks[16], (DEPTH, D_MODEL, D_FF), fan)
    w_down = nrm(ks[17], (DEPTH, D_FF, D_MODEL), D_FF ** -0.5 * beta)
    return {'x_prompt': x_prompt, 'x_sample': x_sample,
            'cache_sb_kv': cache_sb_kv, 'cache_cmp_kv': cache_cmp_kv, 'cache_sel_kv': cache_sel_kv,
            'state_win_kv': state_win_kv, 'page_table': page_table,
            'w_in_sb': w_in_sb, 'w_out_sb': w_out_sb, 'w_in_nsa': w_in_nsa, 'w_out_nsa': w_out_nsa,
            'cmp_pos': cmp_pos, 'w_cmp': w_cmp, 'rel_bias': rel_bias,
            'ln_g': ln_g, 'ln_b': ln_b, 'w_up': w_up, 'w_down': w_down}


def reference(x_prompt, x_sample, cache_sb_kv, cache_cmp_kv, cache_sel_kv, state_win_kv, page_table,
              w_in_sb, w_out_sb, w_in_nsa, w_out_nsa, cmp_pos, w_cmp, rel_bias,
              ln_g, ln_b, w_up, w_down):
    xp, xs = x_prompt, x_sample
    Bp, T = xp.shape[:2]
    Bs, Tn = xs.shape[:2]
    sb_p, sb_s, cmp_p, cmp_s, sel_p, sel_s, win_p, win_s = [], [], [], [], [], [], [], []
    for i in range(DEPTH):
        l = i // 2
        if i % 2 == 0:
            qp, kp, vp = [a.reshape(Bp, T, SB_HEADS, HEAD_DIM) for a in jnp.split(xp @ w_in_sb[l], 3, axis=-1)]
            qs, ks_, vs = [a.reshape(Bs, Tn, SB_HEADS, HEAD_DIM) for a in jnp.split(xs @ w_in_sb[l], 3, axis=-1)]
            mp = _sb_prompt(qp, kp, vp)
            ms = _sb_sample(qs, ks_, vs, cache_sb_kv, l, page_table)
            sb_p.append(jnp.stack([kp, vp], axis=2))
            sb_s.append(jnp.stack([ks_, vs], axis=2))
            w_out = w_out_sb[l]
        else:
            mp, c_p, s_p, w_p = _nsa_prompt(xp @ w_in_nsa[l], cmp_pos[l], w_cmp[l], rel_bias)
            ms, c_s, s_s, w_s = _nsa_sample(xs @ w_in_nsa[l], cache_cmp_kv, cache_sel_kv, state_win_kv, l,
                                            page_table, cmp_pos[l], w_cmp[l], rel_bias)
            cmp_p.append(c_p); sel_p.append(s_p); win_p.append(w_p)
            cmp_s.append(c_s); sel_s.append(s_s); win_s.append(w_s)
            w_out = w_out_nsa[l]
        xp = _layer_norm(DEEPNORM_ALPHA * xp + mp.astype(xp.dtype) @ w_out, ln_g[i, 0], ln_b[i, 0])
        xs = _layer_norm(DEEPNORM_ALPHA * xs + ms.astype(xs.dtype) @ w_out, ln_g[i, 0], ln_b[i, 0])
        xp = _layer_norm(DEEPNORM_ALPHA * xp + _sq_relu_mlp(xp, w_up[i], w_down[i]), ln_g[i, 1], ln_b[i, 1])
        xs = _layer_norm(DEEPNORM_ALPHA * xs + _sq_relu_mlp(xs, w_up[i], w_down[i]), ln_g[i, 1], ln_b[i, 1])
    return (xp, xs, jnp.stack(sb_p), jnp.stack(sb_s), jnp.stack(cmp_p), jnp.stack(cmp_s),
            jnp.stack(sel_p), jnp.stack(sel_s), jnp.stack(win_p), jnp.stack(win_s))
```

```python
import functools
import math

import numpy as np
import jax
import jax.numpy as jnp
from jax import lax
from jax.experimental import pallas as pl
from jax.experimental.pallas import tpu as pltpu

F32 = jnp.float32
BF16 = jnp.bfloat16

HEAD_DIM = 64
NSA_GROUPS = 4
NSA_REP = 4
CMP_BLOCK = 32
SEL_BLOCK = 64
N_SEL = 16
WINDOW = 512
N_BUCKETS = 32
MAX_DISTANCE = 128
LN_EPS = 1e-5
SEL_FORCE = 1e4
NEG = -1e30

LANES = 128
ROW_TILE = 256
SB_TILE = 256
NSA_TQ = 128
SB_PAGES_PER_STEP = 4
CMP_PAGES_PER_STEP = 8
VMEM_LIMIT = 56 * 1024 * 1024

_NT = (((1,), (1,)), ((), ()))


def _dot(a, b):
    return jnp.dot(a, b, preferred_element_type=F32)


def _dot_nt(a, b):
    return lax.dot_general(a, b, _NT, preferred_element_type=F32)


def _split2(x):
    hi = x.astype(BF16)
    lo = (x - hi.astype(F32)).astype(BF16)
    return hi, lo


def _split3(x):
    hi = x.astype(BF16)
    r = x - hi.astype(F32)
    mid = r.astype(BF16)
    lo = (r - mid.astype(F32)).astype(BF16)
    return hi, mid, lo


def _dot_exact(x, m):
    hi, mid, lo = _split3(x)
    return _dot(hi, m) + _dot(mid, m) + _dot(lo, m)


def _softplus(z):
    return jnp.maximum(z, 0.0) + jnp.log1p(jnp.exp(-jnp.abs(z)))


def _layer_norm(y, g, b):
    mu = jnp.mean(y, axis=-1, keepdims=True)
    yc = y - mu
    var = jnp.mean(yc * yc, axis=-1, keepdims=True)
    return yc * lax.rsqrt(var + LN_EPS) * g + b


def _params(*sem):
    return pltpu.CompilerParams(dimension_semantics=sem, vmem_limit_bytes=VMEM_LIMIT)


def _const_spec(a, grid_rank):
    nd = a.ndim
    return pl.BlockSpec(a.shape, lambda *_: (0,) * nd)


def _row_call(body, row_args, const_args, outs, name):
    m = row_args[0].shape[0]
    tm = min(ROW_TILE, m)
    assert m % tm == 0
    in_specs = [pl.BlockSpec((tm, a.shape[1]), lambda i: (i, 0)) for a in row_args]
    in_specs += [_const_spec(a, 1) for a in const_args]
    out_specs = [pl.BlockSpec((tm, o.shape[1]), lambda i: (i, 0)) for o in outs]
    return pl.pallas_call(
        body, grid=(m // tm,), in_specs=in_specs, out_specs=out_specs, out_shape=outs,
        compiler_params=_params("parallel"), name=name)(*row_args, *const_args)


def _sb_proj_body(x_ref, wq_ref, wkv_ref, q_ref, kv_ref, kb_ref, vb_ref):
    xb = x_ref[...].astype(BF16)
    q_ref[...] = (_dot(xb, wq_ref[...]) * HEAD_DIM ** -0.5).astype(BF16)
    kv = _dot(xb, wkv_ref[...])
    kv_ref[...] = kv
    d = kb_ref.shape[1]
    kb_ref[...] = kv[:, :d].astype(BF16)
    vb_ref[...] = kv[:, d:].astype(BF16)


def _sb_proj(x, wq, wkv):
    m, d = x.shape
    outs = [jax.ShapeDtypeStruct((m, d), BF16), jax.ShapeDtypeStruct((m, 2 * d), F32),
            jax.ShapeDtypeStruct((m, d), BF16), jax.ShapeDtypeStruct((m, d), BF16)]
    return _row_call(_sb_proj_body, [x], [wq, wkv], outs, "sb_proj")


def _tail_body(x_ref, m_ref, wo_ref, g1_ref, b1_ref, wu_ref, wd_ref, g2_ref, b2_ref, o_ref, *, alpha):
    y = alpha * x_ref[...] + _dot(m_ref[...].astype(BF16), wo_ref[...])
    y = _layer_norm(y, g1_ref[...], b1_ref[...])
    h = jnp.maximum(_dot(y.astype(BF16), wu_ref[...]), 0.0)
    y = alpha * y + _dot((h * h).astype(BF16), wd_ref[...])
    o_ref[...] = _layer_norm(y, g2_ref[...], b2_ref[...])


def _tail(x, mix, wo, g1, b1, wu, wd, g2, b2, alpha):
    outs = [jax.ShapeDtypeStruct(x.shape, F32)]
    body = functools.partial(_tail_body, alpha=alpha)
    return _row_call(body, [x, mix], [wo, g1, b1, wu, wd, g2, b2], outs, "block_tail")[0]


def _rev_cumsum_matrix(n):
    j = np.arange(n)
    return jnp.asarray(j[:, None] >= j[None, :], BF16)


def _sb_tile(z, u, suf, mask):
    sp = _softplus(z)
    if mask is not None:
        sp = jnp.where(mask, sp, 0.0)
    hi, lo = _split2(sp)
    c = _dot(hi, u) + _dot(lo, u) + suf
    a = jnp.exp(z - c)
    if mask is not None:
        a = jnp.where(mask, a, 0.0)
    return a, c[:, :1]


def _sb_prompt_body(q_ref, k_ref, v_ref, u_ref, o_ref, acc_ref):
    tq = q_ref.shape[0]
    qi = pl.program_id(2)
    lane = lax.broadcasted_iota(jnp.int32, (tq, LANES), 1)
    lo_half = lane < HEAD_DIM
    q2 = q_ref[...]
    zero = jnp.zeros_like(q2)
    qs = (jnp.where(lo_half, q2, zero), jnp.where(lo_half, zero, q2))
    u = u_ref[...]
    row = lax.broadcasted_iota(jnp.int32, (tq, tq), 0)
    col = lax.broadcasted_iota(jnp.int32, (tq, tq), 1)
    strictly_before = col < row
    acc_ref[...] = jnp.zeros_like(acc_ref)

    def tile(j, sufs, mask):
        start = pl.multiple_of(j * tq, tq)
        k2 = k_ref[pl.ds(start, tq), :]
        v2 = v_ref[pl.ds(start, tq), :]
        out = []
        for h in range(2):
            a, suf = _sb_tile(_dot_nt(qs[h], k2), u, sufs[h], mask)
            acc_ref[h] += _dot(a.astype(BF16), v2)
            out.append(suf)
        return tuple(out)

    zero_suf = jnp.zeros((tq, 1), F32)
    sufs = tile(qi, (zero_suf, zero_suf), strictly_before)
    lax.fori_loop(0, qi, lambda jj, s: tile(qi - 1 - jj, s, None), sufs)
    o_ref[...] = jnp.where(lo_half, acc_ref[0], acc_ref[1]).astype(BF16)


def _sb_prompt(q, k, v, batch):
    m, d = q.shape
    t = m // batch
    tq = min(SB_TILE, t)
    assert t % tq == 0 and d % LANES == 0
    nq = t // tq
    u = _rev_cumsum_matrix(tq)
    return pl.pallas_call(
        _sb_prompt_body, grid=(batch, d // LANES, nq),
        in_specs=[pl.BlockSpec((tq, LANES), lambda b, h, i: (b * nq + i, h)),
                  pl.BlockSpec((t, LANES), lambda b, h, i: (b, h)),
                  pl.BlockSpec((t, LANES), lambda b, h, i: (b, h)),
                  pl.BlockSpec((tq, tq), lambda b, h, i: (0, 0))],
        out_specs=pl.BlockSpec((tq, LANES), lambda b, h, i: (b * nq + i, h)),
        out_shape=jax.ShapeDtypeStruct((m, d), BF16),
        scratch_shapes=[pltpu.VMEM((2, tq, LANES), F32)],
        compiler_params=_params("parallel", "parallel", "arbitrary"), name="sb_prompt")(q, k, v, u)


def _sb_sample_body(pt_ref, q_ref, u_ref, *rest, pages_per_step):
    pages = rest[:pages_per_step]
    o_ref, acc_ref, suf_ref = rest[pages_per_step:]
    s = pl.program_id(1)
    d = q_ref.shape[-1]
    heads = d // HEAD_DIM

    @pl.when(s == 0)
    def _():
        acc_ref[...] = jnp.zeros_like(acc_ref)
        suf_ref[...] = jnp.zeros_like(suf_ref)

    own = (lax.broadcasted_iota(jnp.int32, (heads, d), 1) // HEAD_DIM
           == lax.broadcasted_iota(jnp.int32, (heads, d), 0))
    qrow = jnp.broadcast_to(q_ref[0].astype(F32), (heads, d))
    qbd = jnp.where(own, qrow, 0.0).astype(BF16)
    u = u_ref[...]
    acc = acc_ref[...]
    suf = suf_ref[...]
    for p in range(pages_per_step):
        page = pages[p][0]
        kp = page[:, :d].astype(BF16)
        vp = page[:, d:].astype(BF16)
        a, suf = _sb_tile(_dot_nt(qbd, kp), u, suf, None)
        acc = acc + _dot(a.astype(BF16), vp)
    acc_ref[...] = acc
    suf_ref[...] = suf

    @pl.when(s == pl.num_programs(1) - 1)
    def _():
        o_ref[0] = jnp.sum(jnp.where(own, acc, 0.0), axis=0, keepdims=True)


def _sb_sample(q, cache, layer, page_table):
    b, d = q.shape
    n_layers, n_pool, page, two, heads, dh = cache.shape
    assert two == 2 and heads * dh == d
    n_pages = page_table.shape[1]
    pps = math.gcd(SB_PAGES_PER_STEP, n_pages)
    cache2 = cache.reshape(n_layers * n_pool, page, 2 * d)
    base = layer * n_pool

    def page_spec(p):
        return pl.BlockSpec((1, page, 2 * d),
                            lambda i, s, pt: (base + pt[i, n_pages - 1 - (s * pps + p)], 0, 0))

    grid_spec = pltpu.PrefetchScalarGridSpec(
        num_scalar_prefetch=1, grid=(b, n_pages // pps),
        in_specs=[pl.BlockSpec((1, 1, d), lambda i, s, pt: (i, 0, 0)),
                  pl.BlockSpec((page, page), lambda i, s, pt: (0, 0))]
                 + [page_spec(p) for p in range(pps)],
        out_specs=pl.BlockSpec((1, 1, d), lambda i, s, pt: (i, 0, 0)),
        scratch_shapes=[pltpu.VMEM((heads, d), F32), pltpu.VMEM((heads, 1), F32)])
    out = pl.pallas_call(
        functools.partial(_sb_sample_body, pages_per_step=pps), grid_spec=grid_spec,
        out_shape=jax.ShapeDtypeStruct((b, 1, d), F32),
        compiler_params=_params("parallel", "arbitrary"), name="sb_sample")(
            page_table, q.reshape(b, 1, d), _rev_cumsum_matrix(page), *([cache2] * pps))
    return out.reshape(b, d)


KV_W = 2 * NSA_GROUPS * HEAD_DIM
HALF_W = NSA_GROUPS * HEAD_DIM


def _nsa_proj_body(x_ref, wq_ref, wkv_ref, wg_ref, q_ref, cmp_ref, sel_ref, win_ref, selb_ref, winb_ref, g_ref):
    xb = x_ref[...].astype(BF16)
    q_ref[...] = (_dot(xb, wq_ref[...]) * HEAD_DIM ** -0.5).astype(BF16)
    kv = _dot(xb, wkv_ref[...])
    cmp_ref[...] = kv[:, :KV_W]
    sel = kv[:, KV_W:2 * KV_W]
    win = kv[:, 2 * KV_W:]
    sel_ref[...] = sel
    win_ref[...] = win
    selb_ref[...] = sel.astype(BF16)
    winb_ref[...] = win.astype(BF16)
    g_ref[...] = jax.nn.sigmoid(_dot(xb, wg_ref[...]))


def _nsa_proj(x, wq, wkv, wg):
    m, d = x.shape
    outs = [jax.ShapeDtypeStruct((m, d), BF16)] + [jax.ShapeDtypeStruct((m, KV_W), F32)] * 3 \
        + [jax.ShapeDtypeStruct((m, KV_W), BF16)] * 2 + [jax.ShapeDtypeStruct((m, wg.shape[1]), F32)]
    return _row_call(_nsa_proj_body, [x], [wq, wkv, wg], outs, "nsa_proj")


def _compress_accumulate(acc_ref, xb, w_ref, l0, nl):
    for l in range(nl):
        for c in range(2):
            lo = l * KV_W + c * HALF_W
            acc_ref[:, c * HALF_W:(c + 1) * HALF_W] += _dot(xb[:, lo:lo + HALF_W], w_ref[l0 + l, c])


CMP_L_CHUNK = 4


def _compress_body(x_ref, pos_ref, w_ref, o_ref, acc_ref):
    kc = pl.program_id(1)

    @pl.when(kc == 0)
    def _():
        acc_ref[...] = jnp.zeros_like(acc_ref)

    xb = (x_ref[...] + pos_ref[...]).astype(BF16)
    _compress_accumulate(acc_ref, xb, w_ref, 0, CMP_L_CHUNK)

    @pl.when(kc == pl.num_programs(1) - 1)
    def _():
        o_ref[...] = acc_ref[...].astype(BF16)


def _compress(x2, pos_flat, w2):
    rows, width = x2.shape
    tm = min(512, rows)
    assert rows % tm == 0 and CMP_BLOCK % CMP_L_CHUNK == 0
    kw = CMP_L_CHUNK * KV_W
    return pl.pallas_call(
        _compress_body, grid=(rows // tm, width // kw),
        in_specs=[pl.BlockSpec((tm, kw), lambda i, k: (i, k)),
                  pl.BlockSpec((1, kw), lambda i, k: (0, k)),
                  pl.BlockSpec((CMP_L_CHUNK, 2, HALF_W, HALF_W), lambda i, k: (k, 0, 0, 0))],
        out_specs=pl.BlockSpec((tm, KV_W), lambda i, k: (i, 0)),
        out_shape=jax.ShapeDtypeStruct((rows, KV_W), BF16),
        scratch_shapes=[pltpu.VMEM((tm, KV_W), F32)],
        compiler_params=_params("parallel", "arbitrary"), name="nsa_compress")(x2, pos_flat, w2)


def _compress_paged_body(pt_ref, pos_ref, w_ref, *rest, pages_per_step, steps_per_dot):
    pages = rest[:pages_per_step]
    o_ref, x_ref, acc_ref = rest[pages_per_step:]
    s = pl.program_id(1)
    rows_per_page = pages[0].shape[1]
    slot = s % steps_per_dot
    pos = pos_ref[...]
    group = 8 // rows_per_page
    step_rows = pages_per_step * rows_per_page
    r0 = pl.multiple_of(slot * step_rows, step_rows)
    for p in range(0, pages_per_step, group):
        slab = jnp.concatenate([pages[p + k][0] for k in range(group)], axis=0) + pos
        x_ref[pl.ds(pl.multiple_of(r0 + p * rows_per_page, 8), 8), :] = slab

    @pl.when(slot == steps_per_dot - 1)
    def _():
        acc_ref[...] = jnp.zeros_like(acc_ref)
        for l0 in range(0, CMP_BLOCK, CMP_L_CHUNK):
            xb = x_ref[:, l0 * KV_W:(l0 + CMP_L_CHUNK) * KV_W].astype(BF16)
            _compress_accumulate(acc_ref, xb, w_ref, l0, CMP_L_CHUNK)
        o_ref[...] = acc_ref[...].astype(BF16)


def _compress_paged(cache, layer, page_table, pos_flat, w2):
    n_layers, n_pool, page = cache.shape[:3]
    bs, n_pages = page_table.shape
    rpp = page // CMP_BLOCK
    width = CMP_BLOCK * KV_W
    cache2 = cache.reshape(n_layers * n_pool, rpp, width)
    base = layer * n_pool
    pps = math.gcd(CMP_PAGES_PER_STEP, n_pages)
    spd = math.gcd(8, n_pages // pps)
    rows = pps * spd * rpp
    assert 8 % rpp == 0 and (pps * rpp) % 8 == 0
    n_steps = n_pages // pps

    def page_spec(p):
        return pl.BlockSpec((1, rpp, width), lambda i, s, pt: (base + pt[i, s * pps + p], 0, 0))

    grid_spec = pltpu.PrefetchScalarGridSpec(
        num_scalar_prefetch=1, grid=(bs, n_steps),
        in_specs=[pl.BlockSpec((1, width), lambda i, s, pt: (0, 0)),
                  pl.BlockSpec(w2.shape, lambda i, s, pt: (0, 0, 0, 0))]
                 + [page_spec(p) for p in range(pps)],
        out_specs=pl.BlockSpec((rows, KV_W), lambda i, s, pt: (i * (n_steps // spd) + s // spd, 0)),
        scratch_shapes=[pltpu.VMEM((rows, width), F32), pltpu.VMEM((rows, KV_W), F32)])
    body = functools.partial(_compress_paged_body, pages_per_step=pps, steps_per_dot=spd)
    return pl.pallas_call(
        body, grid_spec=grid_spec, out_shape=jax.ShapeDtypeStruct((bs * n_pages * rpp, KV_W), BF16),
        compiler_params=_params("parallel", "arbitrary"), name="nsa_compress_paged")(
            page_table, pos_flat, w2, *([cache2] * pps))


def _rel_bucket(dist):
    dist = jnp.maximum(dist, 0)
    max_exact = N_BUCKETS // 2
    dd = jnp.maximum(dist, 1).astype(F32)
    large = max_exact + (jnp.log(dd / max_exact) / math.log(MAX_DISTANCE / max_exact)
                         * (N_BUCKETS - max_exact)).astype(jnp.int32)
    large = jnp.minimum(large, N_BUCKETS - 1)
    return jnp.where(dist < max_exact, dist, large)


def _bias_of(rel_bias, dist):
    return jnp.moveaxis(rel_bias[_rel_bucket(jnp.asarray(dist, jnp.int32))].astype(F32), -1, 0)


def _bucket_saturates_from(d0, d1):
    dd = np.arange(d0, d1 + 1).astype(np.float32)
    max_exact = N_BUCKETS // 2
    large = max_exact + (np.log(dd / np.float32(max_exact)) / np.float32(math.log(MAX_DISTANCE / max_exact))
                         * np.float32(N_BUCKETS - max_exact)).astype(np.int32)
    return bool(np.all(large >= N_BUCKETS - 1)) and d0 >= max_exact


def _online_update(m_ref, l_ref, acc_ref, i, z, v2):
    m_old = m_ref[i]
    m_new = jnp.maximum(m_old, jnp.max(z, axis=-1, keepdims=True))
    scale = jnp.exp(m_old - m_new)
    p = jnp.exp(z - m_new)
    l_ref[i] = scale * l_ref[i] + jnp.sum(p, axis=-1, keepdims=True)
    acc_ref[i] = scale * acc_ref[i] + _dot(p.astype(BF16), v2)
    m_ref[i] = m_new


def _nsa_prompt_body(q_ref, kc_ref, vc_ref, ks_ref, vs_ref, kw_ref, vw_ref, g_ref, near_ref, far_ref, cb_ref,
                     e_ref, pair_ref, o_ref, qh_ref, selx_ref, m_ref, l_ref, acc_ref, out_ref):
    tq = q_ref.shape[0]
    nc = kc_ref.shape[0]
    n_heads = 2 * NSA_REP
    qi = pl.program_id(2)
    lane = lax.broadcasted_iota(jnp.int32, (tq, LANES), 1)
    rowl = lax.broadcasted_iota(jnp.int32, (tq, LANES), 0)
    lo_half = lane < HEAD_DIM
    half_mask = (lo_half, jnp.logical_not(lo_half))
    gates = g_ref[...]

    def gate(branch, i):
        k = branch * n_heads + i
        return gates[:, k:k + 1]

    for i in range(n_heads):
        hf = i // NSA_REP
        qf = q_ref[:, (i // 2) * LANES:(i // 2 + 1) * LANES].astype(F32)
        if i % 2 != hf:
            qf = pltpu.roll(qf, HEAD_DIM, 1)
        qh_ref[i] = jnp.where(half_mask[hf], qf, 0.0).astype(BF16)

    coln = lax.broadcasted_iota(jnp.int32, (tq, nc), 1)
    rown = lax.broadcasted_iota(jnp.int32, (tq, nc), 0)
    valid_c = qi * tq + rown - (coln * CMP_BLOCK + CMP_BLOCK - 1) >= 0
    kc = kc_ref[...]
    vc = vc_ref[...]
    for hf in range(2):
        imp = jnp.zeros((tq, nc), F32)
        for r in range(NSA_REP):
            i = hf * NSA_REP + r
            z = jnp.where(valid_c, _dot_nt(qh_ref[i], kc) + cb_ref[i], NEG)
            e = jnp.where(valid_c, jnp.exp(z - jnp.max(z, axis=-1, keepdims=True)), 0.0)
            p = e / jnp.maximum(jnp.sum(e, axis=-1, keepdims=True), 1e-30)
            imp = imp + p
            out_ref[i] = gate(0, i) * _dot(p.astype(BF16), vc)
        imp2 = _dot_exact(imp, pair_ref[...])
        qpos = qi * tq + rowl
        cur = qpos // SEL_BLOCK
        forced = (lane == 0) | (lane == cur) | (lane == cur - 1)
        score = jnp.where(lane * SEL_BLOCK > qpos, NEG, jnp.where(forced, SEL_FORCE, imp2))
        n_blocks = ks_ref.shape[0] // SEL_BLOCK
        rank = jnp.zeros((tq, LANES), F32)
        for j in range(n_blocks):
            sj = score[:, j:j + 1]
            beats = (sj > score) | ((sj == score) & (lane > j))
            rank = rank + jnp.where(beats, 1.0, 0.0)
        sel = jnp.where(rank < N_SEL, 1.0, 0.0).astype(BF16)
        selx_ref[hf] = _dot(sel, e_ref[...])

    rowt = lax.broadcasted_iota(jnp.int32, (tq, tq), 0)
    colt = lax.broadcasted_iota(jnp.int32, (tq, tq), 1)

    def reset():
        m_ref[...] = jnp.full_like(m_ref, NEG)
        l_ref[...] = jnp.zeros_like(l_ref)
        acc_ref[...] = jnp.zeros_like(acc_ref)

    def finish(branch):
        for i in range(n_heads):
            out_ref[i] += gate(branch, i) * (acc_ref[i] / l_ref[i])

    def bias_of(i, off):
        return near_ref[i, off] if off < 2 else far_ref[i]

    def sel_tile(j, off, causal):
        start = pl.multiple_of(j * tq, tq)
        k2 = ks_ref[pl.ds(start, tq), :]
        v2 = vs_ref[pl.ds(start, tq), :]
        for hf in range(2):
            keep = selx_ref[hf, :, pl.ds(start, tq)] > 0.5
            if causal:
                keep = keep & (colt <= rowt)
            for r in range(NSA_REP):
                i = hf * NSA_REP + r
                z = jnp.where(keep, _dot_nt(qh_ref[i], k2) + bias_of(i, off), NEG)
                _online_update(m_ref, l_ref, acc_ref, i, z, v2)

    reset()
    sel_tile(qi, 0, True)

    @pl.when(qi >= 1)
    def _():
        sel_tile(qi - 1, 1, False)

    def far_step(jj, carry):
        sel_tile(qi - 2 - jj, 2, False)
        return carry

    lax.fori_loop(0, jnp.maximum(qi - 1, 0), far_step, 0)
    finish(1)

    def win_tile(off):
        start = pl.multiple_of((qi - off) * tq, tq)
        k2 = kw_ref[pl.ds(start, tq), :]
        v2 = vw_ref[pl.ds(start, tq), :]
        dist = off * tq + rowt - colt
        all_valid = off * tq - (tq - 1) >= 0 and off * tq + tq - 1 <= WINDOW
        keep = (dist >= 0) & (dist <= WINDOW)
        for i in range(n_heads):
            z = _dot_nt(qh_ref[i], k2) + bias_of(i, off)
            if not all_valid:
                z = jnp.where(keep, z, NEG)
            _online_update(m_ref, l_ref, acc_ref, i, z, v2)

    reset()
    win_tile(0)
    for off in range(1, (WINDOW + tq - 1) // tq + 1):
        pl.when(qi >= off)(functools.partial(win_tile, off))
    finish(2)

    for c in range(n_heads // 2):
        halves = []
        for i in (2 * c, 2 * c + 1):
            o = out_ref[i]
            if i % 2 != i // NSA_REP:
                o = pltpu.roll(o, HEAD_DIM, 1)
            halves.append(o)
        o_ref[:, c * LANES:(c + 1) * LANES] = jnp.where(lo_half, halves[0], halves[1]).astype(BF16)


def _nsa_prompt(q, kvc, selb, winb, gates, rel_bias, batch):
    m, d = q.shape
    t = m // batch
    tq = NSA_TQ
    nq = t // tq
    nc = t // CMP_BLOCK
    nb = t // SEL_BLOCK
    n_heads = 2 * NSA_REP
    assert t % tq == 0 and nb <= LANES and d == 2 * n_heads * HEAD_DIM and KV_W == 4 * LANES
    assert _bucket_saturates_from(tq + 1, t + WINDOW)

    tl = np.arange(tq)
    near = jnp.stack([_bias_of(rel_bias, o * tq + tl[:, None] - tl[None, :]) for o in range(2)], axis=1)
    far = _bias_of(rel_bias, np.full((1, LANES), 2 * tq))
    cb = _bias_of(rel_bias, np.arange(t)[:, None] - (np.arange(nc) * CMP_BLOCK + CMP_BLOCK - 1)[None, :])
    expand = jnp.asarray(np.arange(LANES)[:, None] == (np.arange(t) // SEL_BLOCK)[None, :], BF16)
    pair = jnp.asarray((np.arange(nc) // (SEL_BLOCK // CMP_BLOCK))[:, None] == np.arange(LANES)[None, :], BF16)

    kv_spec = lambda rows, col0: pl.BlockSpec((rows, LANES), lambda b, gp, i: (b, col0 + gp))
    return pl.pallas_call(
        _nsa_prompt_body, grid=(batch, 2, nq),
        in_specs=[pl.BlockSpec((tq, n_heads * HEAD_DIM), lambda b, gp, i: (b * nq + i, gp)),
                  kv_spec(nc, 0), kv_spec(nc, 2), kv_spec(t, 0), kv_spec(t, 2), kv_spec(t, 0), kv_spec(t, 2),
                  pl.BlockSpec((tq, LANES), lambda b, gp, i: (b * nq + i, gp)),
                  pl.BlockSpec((n_heads, 2, tq, tq), lambda b, gp, i: (gp, 0, 0, 0)),
                  pl.BlockSpec((n_heads, 1, LANES), lambda b, gp, i: (gp, 0, 0)),
                  pl.BlockSpec((n_heads, tq, nc), lambda b, gp, i: (gp, i, 0)),
                  pl.BlockSpec((LANES, t), lambda b, gp, i: (0, 0)),
                  pl.BlockSpec((nc, LANES), lambda b, gp, i: (0, 0))],
        out_specs=pl.BlockSpec((tq, n_heads * HEAD_DIM), lambda b, gp, i: (b * nq + i, gp)),
        out_shape=jax.ShapeDtypeStruct((m, d), BF16),
        scratch_shapes=[pltpu.VMEM((n_heads, tq, LANES), BF16), pltpu.VMEM((2, tq, t), F32),
                        pltpu.VMEM((n_heads, tq, 1), F32), pltpu.VMEM((n_heads, tq, 1), F32),
                        pltpu.VMEM((n_heads, tq, LANES), F32), pltpu.VMEM((n_heads, tq, LANES), F32)],
        compiler_params=_params("parallel", "parallel", "arbitrary"), name="nsa_prompt")(
            q, kvc, kvc, selb, selb, winb, winb, gates, near, far, cb, expand, pair)


def _nsa_select_body(q_ref, kvc_ref, cb_ref, pair_ref, oc_ref, idx_ref, *, past, n_take):
    n_heads = q_ref.shape[1]
    kvc = kvc_ref[...]
    z = _dot_nt(q_ref[0], kvc[:, :HALF_W]) + cb_ref[...]
    e = jnp.exp(z - jnp.max(z, axis=-1, keepdims=True))
    p = e / jnp.maximum(jnp.sum(e, axis=-1, keepdims=True), 1e-30)
    oc = _dot(p.astype(BF16), kvc[:, HALF_W:])
    for g in range(NSA_GROUPS):
        oc_ref[0, g * NSA_REP:(g + 1) * NSA_REP, :] = oc[g * NSA_REP:(g + 1) * NSA_REP,
                                                         g * HEAD_DIM:(g + 1) * HEAD_DIM]
    imp = jnp.concatenate([jnp.sum(p[g * NSA_REP:(g + 1) * NSA_REP], axis=0, keepdims=True)
                           for g in range(NSA_GROUPS)], axis=0)
    imp2 = _dot_exact(imp, pair_ref[...])
    width = imp2.shape[1]
    lane = lax.broadcasted_iota(jnp.int32, (NSA_GROUPS, width), 1)
    lanef = lane.astype(F32)
    cur = past // SEL_BLOCK
    forced = (lane == 0) | (lane == cur) | (lane == cur - 1)
    score = jnp.where(lane * SEL_BLOCK > past, NEG, jnp.where(forced, SEL_FORCE, imp2))
    out_lane = lax.broadcasted_iota(jnp.int32, (NSA_GROUPS, LANES), 1)
    idx = jnp.zeros((NSA_GROUPS, LANES), F32)
    for k in range(n_take):
        best = jnp.max(score, axis=-1, keepdims=True)
        pick = jnp.min(jnp.where(score == best, lanef, float(width)), axis=-1, keepdims=True)
        idx = jnp.where(out_lane == k, pick, idx)
        score = jnp.where(lanef == pick, 2.0 * NEG, score)
    idx_ref[0] = idx.astype(jnp.int32)


def _nsa_select(q16, kvc, cb, past):
    bs, n_heads, _ = q16.shape
    n_cmp = kvc.shape[0] // bs
    n_blocks = past // SEL_BLOCK + 1
    width = -(-n_blocks // LANES) * LANES
    n_take = min(N_SEL, n_blocks)
    pair = jnp.asarray((np.arange(n_cmp) // (SEL_BLOCK // CMP_BLOCK))[:, None] == np.arange(width)[None, :], BF16)
    body = functools.partial(_nsa_select_body, past=past, n_take=n_take)
    return pl.pallas_call(
        body, grid=(bs,),
        in_specs=[pl.BlockSpec((1, n_heads, HALF_W), lambda b: (b, 0, 0)),
                  pl.BlockSpec((n_cmp, KV_W), lambda b: (b, 0)),
                  pl.BlockSpec((n_heads, n_cmp), lambda b: (0, 0)),
                  pl.BlockSpec((n_cmp, width), lambda b: (0, 0))],
        out_specs=[pl.BlockSpec((1, n_heads, HEAD_DIM), lambda b: (b, 0, 0)),
                   pl.BlockSpec((1, NSA_GROUPS, LANES), lambda b: (b, 0, 0))],
        out_shape=[jax.ShapeDtypeStruct((bs, n_heads, HEAD_DIM), F32),
                   jax.ShapeDtypeStruct((bs, NSA_GROUPS, LANES), jnp.int32)],
        compiler_params=_params("parallel"), name="nsa_select")(q16, kvc, cb, pair), n_take


BIAS_TABLE = 2 * LANES


def _lookup_bias(tab, dist):
    n = dist.shape[1]
    onehot = (lax.broadcasted_iota(jnp.int32, (BIAS_TABLE, n), 0) == jnp.minimum(dist, BIAS_TABLE - 1))
    return _dot_exact(tab, jnp.where(onehot, 1.0, 0.0).astype(BF16))


def _softmax_with_new(z, z_new):
    m = jnp.maximum(jnp.max(z, axis=-1, keepdims=True), z_new)
    e = jnp.exp(z - m)
    e_new = jnp.exp(z_new - m)
    den = jnp.maximum(jnp.sum(e, axis=-1, keepdims=True) + e_new, 1e-30)
    return e / den, e_new / den


def _bf16_round(x):
    return x.astype(BF16).astype(F32)


def _nsa_attend_body(pt_ref, ix_ref, q_ref, seln_ref, winn_ref, win_ref, oc_ref, g_ref, tab_ref, wtab_ref, pick_ref,
                     *rest, past, n_take):
    blocks = rest[:n_take]
    o_ref, wout_ref = rest[n_take:]
    b = pl.program_id(0)
    g = pl.program_id(1)
    n_past_blocks = past // SEL_BLOCK
    q4 = q_ref[0, 0]
    q4f = q4.astype(F32)
    tab = tab_ref[0]
    pick = pick_ref[0]
    gates = g_ref[0, 0]

    zs, kvs = [], []
    has_new = False
    lane = lax.broadcasted_iota(jnp.int32, (1, SEL_BLOCK), 1)
    for i in range(n_take):
        v = ix_ref[(b * NSA_GROUPS + g) * n_take + i]
        kvb = blocks[i][0].astype(BF16)
        dist = past - (v * SEL_BLOCK + lane)
        ok = (dist >= 0) & (v < n_past_blocks)
        z = _dot_nt(q4, kvb) + _lookup_bias(tab, jnp.maximum(dist, 0))
        zs.append(jnp.where(ok, z, NEG))
        kvs.append(kvb)
        has_new = jnp.logical_or(has_new, v >= n_past_blocks)
    new_s = _bf16_round(seln_ref[0])
    z_new = jnp.sum(q4f * new_s, axis=-1, keepdims=True) + tab[:, :1]
    z_new = jnp.where(has_new, z_new, NEG)
    m = z_new
    for z in zs:
        m = jnp.maximum(m, jnp.max(z, axis=-1, keepdims=True))
    es = [jnp.exp(z - m) for z in zs]
    e_new = jnp.exp(z_new - m)
    den = e_new
    for e in es:
        den = den + jnp.sum(e, axis=-1, keepdims=True)
    den = jnp.maximum(den, 1e-30)
    o_s = _bf16_round(e_new / den) * new_s
    for e, kvb in zip(es, kvs):
        o_s = o_s + _dot((e / den).astype(BF16), kvb)

    w = win_ref[0]
    wb = w.astype(BF16)
    new_w = _bf16_round(winn_ref[0])
    n_win = w.shape[0]
    zw = _dot_nt(q4, wb) + wtab_ref[0][:, :n_win]
    zw_new = jnp.sum(q4f * new_w, axis=-1, keepdims=True) + wtab_ref[0][:, n_win:n_win + 1]
    pw, pw_new = _softmax_with_new(zw, zw_new)
    o_w = _dot(pw.astype(BF16), wb) + _bf16_round(pw_new) * new_w

    o_sw = gates[:, 1:2] * o_s + gates[:, 2:3] * o_w
    o_ref[0, 0] = gates[:, 0:1] * oc_ref[0, 0] + _dot_exact(o_sw, pick)

    @pl.when(g == 0)
    def _():
        rows = lax.broadcasted_iota(jnp.int32, w.shape, 0)
        wout_ref[0] = jnp.where(rows == n_win - 1, winn_ref[0], pltpu.roll(w, n_win - 1, 0))


def _nsa_attend(q4, sel_new, win_new, state_win, layer, cache_sel, page_table, idx, n_take, oc, gates, rel_bias):
    bs = q4.shape[0]
    n_layers, n_pool, page = cache_sel.shape[:3]
    n_pages = page_table.shape[1]
    past = n_pages * page
    bpp = page // SEL_BLOCK
    n_past_blocks = past // SEL_BLOCK
    n_win = state_win.shape[2]
    assert _bucket_saturates_from(BIAS_TABLE - 1, past + SEL_BLOCK)
    cache2 = cache_sel.reshape(n_layers * n_pool * bpp, SEL_BLOCK, KV_W)
    base = layer * n_pool * bpp
    win2 = state_win.reshape(state_win.shape[0] * bs, n_win, KV_W)
    tab = _bias_of(rel_bias, np.arange(BIAS_TABLE)).reshape(NSA_GROUPS, NSA_REP, BIAS_TABLE)
    wpad = -(-(n_win + 1) // LANES) * LANES
    wdist = np.maximum(n_win - np.arange(wpad), 0)
    wtab = _bias_of(rel_bias, wdist).reshape(NSA_GROUPS, NSA_REP, wpad)
    lanes = np.arange(KV_W)[None, :, None]
    pick = jnp.asarray(lanes == HALF_W + np.arange(NSA_GROUPS)[:, None, None] * HEAD_DIM
                       + np.arange(HEAD_DIM)[None, None, :], BF16)

    def block_spec(i):
        def index(b, g, pt, ix):
            v = jnp.minimum(ix[(b * NSA_GROUPS + g) * n_take + i], n_past_blocks - 1)
            return (base + pt[b, v // bpp] * bpp + v % bpp, 0, 0)
        return pl.BlockSpec((1, SEL_BLOCK, KV_W), index)

    grid_spec = pltpu.PrefetchScalarGridSpec(
        num_scalar_prefetch=2, grid=(bs, NSA_GROUPS),
        in_specs=[pl.BlockSpec((1, 1, NSA_REP, KV_W), lambda b, g, pt, ix: (b, g, 0, 0)),
                  pl.BlockSpec((1, 1, KV_W), lambda b, g, pt, ix: (b, 0, 0)),
                  pl.BlockSpec((1, 1, KV_W), lambda b, g, pt, ix: (b, 0, 0)),
                  pl.BlockSpec((1, n_win, KV_W), lambda b, g, pt, ix: (layer * bs + b, 0, 0)),
                  pl.BlockSpec((1, 1, NSA_REP, HEAD_DIM), lambda b, g, pt, ix: (b, g, 0, 0)),
                  pl.BlockSpec((1, 1, NSA_REP, LANES), lambda b, g, pt, ix: (b, g, 0, 0)),
                  pl.BlockSpec((1, NSA_REP, BIAS_TABLE), lambda b, g, pt, ix: (g, 0, 0)),
                  pl.BlockSpec((1, NSA_REP, wpad), lambda b, g, pt, ix: (g, 0, 0)),
                  pl.BlockSpec((1, KV_W, HEAD_DIM), lambda b, g, pt, ix: (g, 0, 0))]
                 + [block_spec(i) for i in range(n_take)],
        out_specs=[pl.BlockSpec((1, 1, NSA_REP, HEAD_DIM), lambda b, g, pt, ix: (b, g, 0, 0)),
                   pl.BlockSpec((1, n_win, KV_W), lambda b, g, pt, ix: (b, 0, 0))])
    body = functools.partial(_nsa_attend_body, past=past, n_take=n_take)
    return pl.pallas_call(
        body, grid_spec=grid_spec,
        out_shape=[jax.ShapeDtypeStruct((bs, NSA_GROUPS, NSA_REP, HEAD_DIM), F32),
                   jax.ShapeDtypeStruct((bs, n_win, KV_W), F32)],
        compiler_params=_params("parallel", "arbitrary"), name="nsa_attend")(
            page_table, idx, q4, sel_new, win_new, win2, oc, gates, tab, wtab, pick, *([cache2] * n_take))


def _compress_weights(cmp_pos, w_cmp):
    pos = jnp.broadcast_to(jnp.swapaxes(cmp_pos, 0, 1)[:, :, None, :], (CMP_BLOCK, 2, NSA_GROUPS, HEAD_DIM))
    eye = jnp.eye(NSA_GROUPS, dtype=w_cmp.dtype)
    w2 = jnp.einsum('clde,gh->lcgdhe', w_cmp, eye).reshape(CMP_BLOCK, 2, HALF_W, HALF_W)
    return pos.reshape(1, CMP_BLOCK * KV_W), w2.astype(BF16)


def _nsa_gate_weights(wg, paired):
    d = wg.shape[0]
    w = wg.reshape(d, 3, NSA_GROUPS, NSA_REP)
    if paired:
        w = w.reshape(d, 3, 2, 2 * NSA_REP).transpose(0, 2, 1, 3).reshape(d, 2, 3 * 2 * NSA_REP)
        w = jnp.pad(w, ((0, 0), (0, 0), (0, LANES - w.shape[2])))
        return w.reshape(d, 2 * LANES).astype(BF16)
    return jnp.pad(wg, ((0, 0), (0, LANES - wg.shape[1]))).astype(BF16)


def kernel(x_prompt, x_sample, cache_sb_kv, cache_cmp_kv, cache_sel_kv, state_win_kv, page_table,
           w_in_sb, w_out_sb, w_in_nsa, w_out_nsa, cmp_pos, w_cmp, rel_bias, ln_g, ln_b, w_up, w_down):
    bp, t, d = x_prompt.shape
    bs, tn, _ = x_sample.shape
    assert tn == 1 and d == NSA_GROUPS * NSA_REP * HEAD_DIM
    depth = ln_g.shape[0]
    alpha = (2 * depth) ** 0.25
    n_pages = page_table.shape[1]
    page = cache_sb_kv.shape[2]
    past = n_pages * page
    n_heads = d // HEAD_DIM
    xp = x_prompt.reshape(bp * t, d)
    xs = x_sample.reshape(bs, d)
    sb_p, sb_s, cmp_p, cmp_s, sel_p, sel_s, win_p, win_s = [], [], [], [], [], [], [], []
    for i in range(depth):
        l = i // 2
        if i % 2 == 0:
            wq = w_in_sb[l][:, :d].astype(BF16)
            wkv = w_in_sb[l][:, d:].astype(BF16)
            q, kv, kb, vb = _sb_proj(xp, wq, wkv)
            mp = _sb_prompt(q, kb, vb, bp)
            qs, kvs, _, _ = _sb_proj(xs, wq, wkv)
            ms = _sb_sample(qs, cache_sb_kv, l, page_table)
            sb_p.append(kv.reshape(bp, t, 2, n_heads, HEAD_DIM))
            sb_s.append(kvs.reshape(bs, tn, 2, n_heads, HEAD_DIM))
            w_out = w_out_sb[l]
        else:
            w = w_in_nsa[l]
            wq = w[:, :d].astype(BF16)
            wkv = w[:, d:d + 3 * KV_W].astype(BF16)
            wg = w[:, d + 3 * KV_W:]
            pos_flat, w2 = _compress_weights(cmp_pos[l], w_cmp[l])
            q, c_p, s_p, w_p, selb, winb, gates = _nsa_proj(xp, wq, wkv, _nsa_gate_weights(wg, True))
            kvc = _compress(c_p.reshape(bp * t // CMP_BLOCK, CMP_BLOCK * KV_W), pos_flat, w2)
            mp = _nsa_prompt(q, kvc, selb, winb, gates, rel_bias, bp)
            w_buf = min(WINDOW, t)
            cmp_p.append(c_p.reshape(bp, t, 2, NSA_GROUPS, HEAD_DIM))
            sel_p.append(s_p.reshape(bp, t, 2, NSA_GROUPS, HEAD_DIM))
            win_p.append(w_p.reshape(bp, t, 2, NSA_GROUPS, HEAD_DIM)[:, t - w_buf:])
            qs, c_s, s_s, w_s, _, _, gates_s = _nsa_proj(xs, wq, wkv, _nsa_gate_weights(wg, False))
            kvc_s = _compress_paged(cache_cmp_kv, l, page_table, pos_flat, w2)
            qh = qs.reshape(bs, NSA_GROUPS, NSA_REP, 1, HEAD_DIM)
            eye = jnp.eye(NSA_GROUPS, dtype=qs.dtype)[None, :, None, :, None]
            q16 = (qh * eye).reshape(bs, n_heads, HALF_W)
            q4 = jnp.pad(q16, ((0, 0), (0, 0), (0, HALF_W))).reshape(bs, NSA_GROUPS, NSA_REP, KV_W)
            n_cmp = past // CMP_BLOCK
            cb_s = _bias_of(rel_bias, past - (np.arange(n_cmp) * CMP_BLOCK + CMP_BLOCK - 1))
            (oc, idx), n_take = _nsa_select(q16, kvc_s, cb_s, past)
            g3 = gates_s[:, :3 * n_heads].reshape(bs, 3, NSA_GROUPS, NSA_REP).transpose(0, 2, 3, 1)
            g3 = jnp.pad(g3, ((0, 0), (0, 0), (0, 0), (0, LANES - 3)))
            o4, wnew = _nsa_attend(q4, s_s.reshape(bs, 1, KV_W), w_s.reshape(bs, 1, KV_W), state_win_kv, l,
                                   cache_sel_kv, page_table, idx[:, :, :n_take].reshape(-1), n_take,
                                   oc.reshape(bs, NSA_GROUPS, NSA_REP, HEAD_DIM), g3, rel_bias)
            ms = o4.reshape(bs, d)
            cmp_s.append(c_s.reshape(bs, tn, 2, NSA_GROUPS, HEAD_DIM))
            sel_s.append(s_s.reshape(bs, tn, 2, NSA_GROUPS, HEAD_DIM))
            win_s.append(wnew.reshape(bs, -1, 2, NSA_GROUPS, HEAD_DIM))
            w_out = w_out_nsa[l]
        tail_args = (w_out.astype(BF16), ln_g[i, 0][None], ln_b[i, 0][None], w_up[i].astype(BF16),
                     w_down[i].astype(BF16), ln_g[i, 1][None], ln_b[i, 1][None], alpha)
        xp = _tail(xp, mp, *tail_args)
        xs = _tail(xs, ms, *tail_args)
    return (xp.reshape(bp, t, d), xs.reshape(bs, tn, d), jnp.stack(sb_p), jnp.stack(sb_s), jnp.stack(cmp_p),
            jnp.stack(cmp_s), jnp.stack(sel_p), jnp.stack(sel_s), jnp.stack(win_p), jnp.stack(win_s))
```

```python
import functools
import math

import numpy as np
import jax
import jax.numpy as jnp
from jax import lax
from jax.experimental import pallas as pl
from jax.experimental.pallas import tpu as pltpu

F32 = jnp.float32
BF16 = jnp.bfloat16

HEAD_DIM = 64
NSA_GROUPS = 4
NSA_REP = 4
CMP_BLOCK = 32
SEL_BLOCK = 64
N_SEL = 16
WINDOW = 512
N_BUCKETS = 32
MAX_DISTANCE = 128
LN_EPS = 1e-5
SEL_FORCE = 1e4
NEG = -1e30

LANES = 128
ROW_TILE = 256
SB_TILE = 256
NSA_TQ = 128
SB_PAGES_PER_STEP = 4
CMP_PAGES_PER_STEP = 8
VMEM_LIMIT = 56 * 1024 * 1024

KV_W = 2 * NSA_GROUPS * HEAD_DIM
HALF_W = NSA_GROUPS * HEAD_DIM

_NT = (((1,), (1,)), ((), ()))


def _dot(a, b):
    return jnp.dot(a, b, preferred_element_type=F32)


def _dot_nt(a, b):
    return lax.dot_general(a, b, _NT, preferred_element_type=F32)


def _split2(x):
    hi = x.astype(BF16)
    lo = (x - hi.astype(F32)).astype(BF16)
    return hi, lo


def _split3(x):
    hi = x.astype(BF16)
    r = x - hi.astype(F32)
    mid = r.astype(BF16)
    lo = (r - mid.astype(F32)).astype(BF16)
    return hi, mid, lo


def _dot_exact(x, m):
    hi, mid, lo = _split3(x)
    return _dot(hi, m) + _dot(mid, m) + _dot(lo, m)


def _softplus(z):
    return jnp.maximum(z, 0.0) + jnp.log1p(jnp.exp(-jnp.abs(z)))


def _layer_norm(y, g, b):
    mu = jnp.mean(y, axis=-1, keepdims=True)
    yc = y - mu
    var = jnp.mean(yc * yc, axis=-1, keepdims=True)
    return yc * lax.rsqrt(var + LN_EPS) * g + b


def _bf16_round(x):
    return x.astype(BF16).astype(F32)


def _params(*sem):
    return pltpu.CompilerParams(dimension_semantics=sem, vmem_limit_bytes=VMEM_LIMIT)


def _const_spec(a):
    nd = a.ndim
    return pl.BlockSpec(a.shape, lambda *_: (0,) * nd)


def _row_call(body, row_args, const_args, outs, name, t_outs=(), seq_len=None):
    m = row_args[0].shape[0]
    tm = min(ROW_TILE, m)
    assert m % tm == 0
    in_specs = [pl.BlockSpec((tm, a.shape[1]), lambda i: (i, 0)) for a in row_args]
    in_specs += [_const_spec(a) for a in const_args]
    out_specs = [pl.BlockSpec((tm, o.shape[1]), lambda i: (i, 0)) for o in outs]
    if t_outs:
        nt = seq_len // tm
        assert seq_len % tm == 0
        out_specs += [pl.BlockSpec((1, o.shape[1], tm), lambda i: (i // nt, 0, i % nt)) for o in t_outs]
    return pl.pallas_call(
        body, grid=(m // tm,), in_specs=in_specs, out_specs=out_specs, out_shape=list(outs) + list(t_outs),
        compiler_params=_params("parallel"), name=name)(*row_args, *const_args)


def _sb_proj_rows_body(x_ref, wq_ref, wkv_ref, q_ref, kv_ref):
    xb = x_ref[...].astype(BF16)
    q_ref[...] = (_dot(xb, wq_ref[...]) * HEAD_DIM ** -0.5).astype(BF16)
    kv_ref[...] = _dot(xb, wkv_ref[...])


def _sb_proj_rows(x, wq, wkv):
    m, d = x.shape
    outs = [jax.ShapeDtypeStruct((m, d), BF16), jax.ShapeDtypeStruct((m, 2 * d), F32)]
    return _row_call(_sb_proj_rows_body, [x], [wq, wkv], outs, "sb_proj_rows")


def _sb_proj_body(x_ref, wq_ref, wv_ref, wkvt_ref, q_ref, v_ref, kvt_ref, kt_ref):
    xb = x_ref[...].astype(BF16)
    q_ref[...] = (_dot(xb, wq_ref[...]) * HEAD_DIM ** -0.5).astype(BF16)
    v_ref[...] = _dot(xb, wv_ref[...]).astype(BF16)
    kvt = _dot_nt(wkvt_ref[...], xb)
    kvt_ref[0] = kvt
    kt_ref[0] = kvt[:kt_ref.shape[1]].astype(BF16)


def _sb_proj(x, wq, wv, wkvt, batch):
    m, d = x.shape
    t = m // batch
    outs = [jax.ShapeDtypeStruct((m, d), BF16), jax.ShapeDtypeStruct((m, d), BF16)]
    t_outs = [jax.ShapeDtypeStruct((batch, 2 * d, t), F32), jax.ShapeDtypeStruct((batch, d, t), BF16)]
    return _row_call(_sb_proj_body, [x], [wq, wv, wkvt], outs, "sb_proj", t_outs, t)


def _tail_body(x_ref, m_ref, wo_ref, g1_ref, b1_ref, wu_ref, wd_ref, g2_ref, b2_ref, o_ref, *, alpha):
    y = alpha * x_ref[...] + _dot(m_ref[...].astype(BF16), wo_ref[...])
    y = _layer_norm(y, g1_ref[...], b1_ref[...])
    h = jnp.maximum(_dot(y.astype(BF16), wu_ref[...]), 0.0)
    y = alpha * y + _dot((h * h).astype(BF16), wd_ref[...])
    o_ref[...] = _layer_norm(y, g2_ref[...], b2_ref[...])


def _tail(x, mix, wo, g1, b1, wu, wd, g2, b2, alpha):
    outs = [jax.ShapeDtypeStruct(x.shape, F32)]
    body = functools.partial(_tail_body, alpha=alpha)
    return _row_call(body, [x, mix], [wo, g1, b1, wu, wd, g2, b2], outs, "block_tail")[0]


def _rev_cumsum_matrix(n):
    j = np.arange(n)
    return jnp.asarray(j[:, None] >= j[None, :], BF16)


def _sb_tile(z, u, suf, mask):
    sp = _softplus(z)
    if mask is not None:
        sp = jnp.where(mask, sp, 0.0)
    hi, lo = _split2(sp)
    c = _dot(hi, u) + _dot(lo, u) + suf
    a = jnp.exp(z - c)
    if mask is not None:
        a = jnp.where(mask, a, 0.0)
    return a, c[:, :1]


def _sb_prompt_body(q_ref, kt_ref, v_ref, u_ref, o_ref, acc_ref):
    tq = q_ref.shape[0]
    qi = pl.program_id(2)
    lane = lax.broadcasted_iota(jnp.int32, (tq, LANES), 1)
    lo_half = lane < HEAD_DIM
    q2 = q_ref[...]
    zero = jnp.zeros_like(q2)
    qs = (jnp.where(lo_half, q2, zero), jnp.where(lo_half, zero, q2))
    u = u_ref[...]
    row = lax.broadcasted_iota(jnp.int32, (tq, tq), 0)
    col = lax.broadcasted_iota(jnp.int32, (tq, tq), 1)
    strictly_before = col < row
    acc_ref[...] = jnp.zeros_like(acc_ref)

    def tile(j, sufs, mask):
        start = pl.multiple_of(j * tq, tq)
        kt2 = kt_ref[0, :, pl.ds(start, tq)]
        v2 = v_ref[pl.ds(start, tq), :]
        out = []
        for h in range(2):
            a, suf = _sb_tile(_dot(qs[h], kt2), u, sufs[h], mask)
            acc_ref[h] += _dot(a.astype(BF16), v2)
            out.append(suf)
        return tuple(out)

    zero_suf = jnp.zeros((tq, 1), F32)
    sufs = tile(qi, (zero_suf, zero_suf), strictly_before)
    lax.fori_loop(0, qi, lambda jj, s: tile(qi - 1 - jj, s, None), sufs)
    o_ref[...] = jnp.where(lo_half, acc_ref[0], acc_ref[1]).astype(BF16)


def _sb_prompt(q, kt, v, batch):
    m, d = q.shape
    t = m // batch
    tq = min(SB_TILE, t)
    assert t % tq == 0 and d % LANES == 0
    nq = t // tq
    u = _rev_cumsum_matrix(tq)
    return pl.pallas_call(
        _sb_prompt_body, grid=(batch, d // LANES, nq),
        in_specs=[pl.BlockSpec((tq, LANES), lambda b, h, i: (b * nq + i, h)),
                  pl.BlockSpec((1, LANES, t), lambda b, h, i: (b, h, 0)),
                  pl.BlockSpec((t, LANES), lambda b, h, i: (b, h)),
                  pl.BlockSpec((tq, tq), lambda b, h, i: (0, 0))],
        out_specs=pl.BlockSpec((tq, LANES), lambda b, h, i: (b * nq + i, h)),
        out_shape=jax.ShapeDtypeStruct((m, d), BF16),
        scratch_shapes=[pltpu.VMEM((2, tq, LANES), F32)],
        compiler_params=_params("parallel", "parallel", "arbitrary"), name="sb_prompt")(q, kt, v, u)


def _position_minor_pages(cache):
    n_layers, n_pool, page = cache.shape[:3]
    feat = cache.shape[3] * cache.shape[4] * cache.shape[5]
    return jnp.transpose(cache, (0, 1, 3, 4, 5, 2)).reshape(n_layers * n_pool, feat, page)


def _sb_sample_body(pt_ref, q_ref, u_ref, *rest, pages_per_step):
    pages = rest[:pages_per_step]
    o_ref, acc_ref, suf_ref = rest[pages_per_step:]
    s = pl.program_id(1)
    d = q_ref.shape[-1]
    heads = d // HEAD_DIM

    @pl.when(s == 0)
    def _():
        acc_ref[...] = jnp.zeros_like(acc_ref)
        suf_ref[...] = jnp.zeros_like(suf_ref)

    own = (lax.broadcasted_iota(jnp.int32, (heads, d), 1) // HEAD_DIM
           == lax.broadcasted_iota(jnp.int32, (heads, d), 0))
    qrow = jnp.broadcast_to(q_ref[0].astype(F32), (heads, d))
    qbd = jnp.where(own, qrow, 0.0).astype(BF16)
    u = u_ref[...]
    acc = acc_ref[...]
    suf = suf_ref[...]
    for p in range(pages_per_step):
        kt = pages[p][0, :d, :].astype(BF16)
        vt = pages[p][0, d:, :].astype(BF16)
        a, suf = _sb_tile(_dot(qbd, kt), u, suf, None)
        acc = acc + _dot_nt(a.astype(BF16), vt)
    acc_ref[...] = acc
    suf_ref[...] = suf

    @pl.when(s == pl.num_programs(1) - 1)
    def _():
        o_ref[0] = jnp.sum(jnp.where(own, acc, 0.0), axis=0, keepdims=True)


def _sb_sample(q, cache, layer, page_table):
    b, d = q.shape
    n_pool, page = cache.shape[1:3]
    assert cache.shape[3] * cache.shape[4] * cache.shape[5] == 2 * d
    n_pages = page_table.shape[1]
    pps = math.gcd(SB_PAGES_PER_STEP, n_pages)
    pages = _position_minor_pages(cache)
    base = layer * n_pool

    def page_spec(p):
        return pl.BlockSpec((1, 2 * d, page),
                            lambda i, s, pt: (base + pt[i, n_pages - 1 - (s * pps + p)], 0, 0))

    grid_spec = pltpu.PrefetchScalarGridSpec(
        num_scalar_prefetch=1, grid=(b, n_pages // pps),
        in_specs=[pl.BlockSpec((1, 1, d), lambda i, s, pt: (i, 0, 0)),
                  pl.BlockSpec((page, page), lambda i, s, pt: (0, 0))]
                 + [page_spec(p) for p in range(pps)],
        out_specs=pl.BlockSpec((1, 1, d), lambda i, s, pt: (i, 0, 0)),
        scratch_shapes=[pltpu.VMEM((d // HEAD_DIM, d), F32), pltpu.VMEM((d // HEAD_DIM, 1), F32)])
    out = pl.pallas_call(
        functools.partial(_sb_sample_body, pages_per_step=pps), grid_spec=grid_spec,
        out_shape=jax.ShapeDtypeStruct((b, 1, d), F32),
        compiler_params=_params("parallel", "arbitrary"), name="sb_sample")(
            page_table, q.reshape(b, 1, d), _rev_cumsum_matrix(page), *([pages] * pps))
    return out.reshape(b, d)


def _nsa_proj_rows_body(x_ref, wq_ref, wkv_ref, wg_ref, q_ref, kv_ref, g_ref):
    xb = x_ref[...].astype(BF16)
    q_ref[...] = (_dot(xb, wq_ref[...]) * HEAD_DIM ** -0.5).astype(BF16)
    kv_ref[...] = _dot(xb, wkv_ref[...])
    g_ref[...] = jax.nn.sigmoid(_dot(xb, wg_ref[...]))


def _nsa_proj_rows(x, wq, wkv, wg):
    m, d = x.shape
    outs = [jax.ShapeDtypeStruct((m, d), BF16), jax.ShapeDtypeStruct((m, wkv.shape[1]), F32),
            jax.ShapeDtypeStruct((m, wg.shape[1]), F32)]
    return _row_call(_nsa_proj_rows_body, [x], [wq, wkv, wg], outs, "nsa_proj_rows")


def _nsa_proj_body(x_ref, wq_ref, wc_ref, wg_ref, wkvt_ref, q_ref, cmp_ref, g_ref,
                   cmpt_ref, selt_ref, wint_ref, seltb_ref, wintb_ref):
    xb = x_ref[...].astype(BF16)
    q_ref[...] = (_dot(xb, wq_ref[...]) * HEAD_DIM ** -0.5).astype(BF16)
    cmp_ref[...] = _dot(xb, wc_ref[...])
    g_ref[...] = jax.nn.sigmoid(_dot(xb, wg_ref[...]))
    kvt = _dot_nt(wkvt_ref[...], xb)
    cmpt_ref[0] = kvt[:KV_W]
    sel = kvt[KV_W:2 * KV_W]
    win = kvt[2 * KV_W:]
    selt_ref[0] = sel
    wint_ref[0] = win
    seltb_ref[0] = sel.astype(BF16)
    wintb_ref[0] = win.astype(BF16)


def _nsa_proj(x, wq, wc, wg, wkvt, batch):
    m, d = x.shape
    t = m // batch
    outs = [jax.ShapeDtypeStruct((m, d), BF16), jax.ShapeDtypeStruct((m, KV_W), F32),
            jax.ShapeDtypeStruct((m, wg.shape[1]), F32)]
    t_outs = [jax.ShapeDtypeStruct((batch, KV_W, t), F32)] * 3 + [jax.ShapeDtypeStruct((batch, KV_W, t), BF16)] * 2
    return _row_call(_nsa_proj_body, [x], [wq, wc, wg, wkvt], outs, "nsa_proj", t_outs, t)


CMP_L_CHUNK = 4


def _compress_body(x_ref, pos_ref, w_ref, o_ref, acc_ref):
    kc = pl.program_id(1)

    @pl.when(kc == 0)
    def _():
        acc_ref[...] = jnp.zeros_like(acc_ref)

    xb = (x_ref[...] + pos_ref[...]).astype(BF16)
    for l in range(CMP_L_CHUNK):
        for c in range(2):
            lo = l * KV_W + c * HALF_W
            acc_ref[:, c * HALF_W:(c + 1) * HALF_W] += _dot(xb[:, lo:lo + HALF_W], w_ref[l, c])

    @pl.when(kc == pl.num_programs(1) - 1)
    def _():
        o_ref[...] = acc_ref[...].astype(BF16)


def _compress(x2, pos_flat, w2):
    rows, width = x2.shape
    tm = min(512, rows)
    assert rows % tm == 0 and CMP_BLOCK % CMP_L_CHUNK == 0
    kw = CMP_L_CHUNK * KV_W
    return pl.pallas_call(
        _compress_body, grid=(rows // tm, width // kw),
        in_specs=[pl.BlockSpec((tm, kw), lambda i, k: (i, k)),
                  pl.BlockSpec((1, kw), lambda i, k: (0, k)),
                  pl.BlockSpec((CMP_L_CHUNK, 2, HALF_W, HALF_W), lambda i, k: (k, 0, 0, 0))],
        out_specs=pl.BlockSpec((tm, KV_W), lambda i, k: (i, 0)),
        out_shape=jax.ShapeDtypeStruct((rows, KV_W), BF16),
        scratch_shapes=[pltpu.VMEM((tm, KV_W), F32)],
        compiler_params=_params("parallel", "arbitrary"), name="nsa_compress")(x2, pos_flat, w2)


def _compress_paged_body(pt_ref, pos_ref, w_ref, *rest, pages_per_step, steps_per_dot):
    pages = rest[:pages_per_step]
    o_ref, x_ref, acc_ref = rest[pages_per_step:]
    s = pl.program_id(1)
    page = pages[0].shape[2]
    slot = s % steps_per_dot
    pos = pos_ref[...]
    for p in range(pages_per_step):
        xt = pages[p][0].T + pos
        r0 = pl.multiple_of((slot * pages_per_step + p) * page, page)
        for k in range(KV_W // LANES):
            x_ref[k, pl.ds(r0, page), :] = xt[:, k * LANES:(k + 1) * LANES]

    @pl.when(slot == steps_per_dot - 1)
    def _():
        rows = acc_ref.shape[0]
        acc_ref[...] = jnp.zeros_like(acc_ref)
        per_c = HALF_W // LANES
        for l in range(CMP_BLOCK):
            for c in range(2):
                xl = jnp.concatenate([x_ref[c * per_c + k, pl.ds(l, rows, stride=CMP_BLOCK), :]
                                      for k in range(per_c)], axis=1)
                acc_ref[:, c * HALF_W:(c + 1) * HALF_W] += _dot(xl.astype(BF16), w_ref[l, c])
        o_ref[...] = acc_ref[...].astype(BF16)


def _compress_paged(cache, layer, page_table, pos_rows, w2):
    n_pool, page = cache.shape[1:3]
    bs, n_pages = page_table.shape
    rpp = page // CMP_BLOCK
    pages = _position_minor_pages(cache)
    base = layer * n_pool
    pps = math.gcd(CMP_PAGES_PER_STEP, n_pages)
    spd = math.gcd(8, n_pages // pps)
    rows = pps * spd * rpp
    n_steps = n_pages // pps
    pos_page = jnp.tile(pos_rows, (rpp, 1))

    def page_spec(p):
        return pl.BlockSpec((1, KV_W, page), lambda i, s, pt: (base + pt[i, s * pps + p], 0, 0))

    grid_spec = pltpu.PrefetchScalarGridSpec(
        num_scalar_prefetch=1, grid=(bs, n_steps),
        in_specs=[pl.BlockSpec((page, KV_W), lambda i, s, pt: (0, 0)),
                  pl.BlockSpec(w2.shape, lambda i, s, pt: (0, 0, 0, 0))]
                 + [page_spec(p) for p in range(pps)],
        out_specs=pl.BlockSpec((rows, KV_W), lambda i, s, pt: (i * (n_steps // spd) + s // spd, 0)),
        scratch_shapes=[pltpu.VMEM((KV_W // LANES, pps * spd * page, LANES), F32), pltpu.VMEM((rows, KV_W), F32)])
    body = functools.partial(_compress_paged_body, pages_per_step=pps, steps_per_dot=spd)
    return pl.pallas_call(
        body, grid_spec=grid_spec, out_shape=jax.ShapeDtypeStruct((bs * n_pages * rpp, KV_W), BF16),
        compiler_params=_params("parallel", "arbitrary"), name="nsa_compress_paged")(
            page_table, pos_page, w2, *([pages] * pps))


def _rel_bucket(dist):
    dist = jnp.maximum(dist, 0)
    max_exact = N_BUCKETS // 2
    dd = jnp.maximum(dist, 1).astype(F32)
    large = max_exact + (jnp.log(dd / max_exact) / math.log(MAX_DISTANCE / max_exact)
                         * (N_BUCKETS - max_exact)).astype(jnp.int32)
    large = jnp.minimum(large, N_BUCKETS - 1)
    return jnp.where(dist < max_exact, dist, large)


def _bias_of(rel_bias, dist):
    return jnp.moveaxis(rel_bias[_rel_bucket(jnp.asarray(dist, jnp.int32))].astype(F32), -1, 0)


def _bucket_saturates_from(d0, d1):
    dd = np.arange(d0, d1 + 1).astype(np.float32)
    max_exact = N_BUCKETS // 2
    large = max_exact + (np.log(dd / np.float32(max_exact)) / np.float32(math.log(MAX_DISTANCE / max_exact))
                         * np.float32(N_BUCKETS - max_exact)).astype(np.int32)
    return bool(np.all(large >= N_BUCKETS - 1)) and d0 >= max_exact


def _nsa_prompt_body(q_ref, kc_ref, vc_ref, ks_ref, vs_ref, kw_ref, vw_ref, g_ref, near_ref, far_ref, cb_ref,
                     e_ref, pair_ref, o_ref, qh_ref, selx_ref, zs_ref, mx_ref, den_ref, acc_ref, out_ref):
    tq = q_ref.shape[0]
    nc = kc_ref.shape[0]
    rep = NSA_REP
    n_heads = 2 * rep
    qi = pl.program_id(2)
    lane = lax.broadcasted_iota(jnp.int32, (tq, LANES), 1)
    rowl = lax.broadcasted_iota(jnp.int32, (tq, LANES), 0)
    lo_half = lane < HEAD_DIM
    half_mask = (lo_half, jnp.logical_not(lo_half))
    gates = g_ref[...]

    def gate(branch, i):
        k = branch * n_heads + i
        return gates[:, k:k + 1]

    def rows(r):
        return slice(r * tq, (r + 1) * tq)

    for i in range(n_heads):
        hf, r = divmod(i, rep)
        qf = q_ref[:, (i // 2) * LANES:(i // 2 + 1) * LANES].astype(F32)
        if i % 2 != hf:
            qf = pltpu.roll(qf, HEAD_DIM, 1)
        qh_ref[hf, rows(r), :] = jnp.where(half_mask[hf], qf, 0.0).astype(BF16)

    coln = lax.broadcasted_iota(jnp.int32, (tq, nc), 1)
    rown = lax.broadcasted_iota(jnp.int32, (tq, nc), 0)
    valid_c = qi * tq + rown - (coln * CMP_BLOCK + CMP_BLOCK - 1) >= 0
    m_rel = (qi + 1) * (tq // CMP_BLOCK) - 1 - coln
    n_near = cb_ref.shape[2] - 1
    kc = kc_ref[...]
    vc = vc_ref[...]
    for hf in range(2):
        imp = jnp.zeros((tq, nc), F32)
        for r in range(rep):
            i = hf * rep + r
            cbias = jnp.broadcast_to(cb_ref[i][:, n_near:n_near + 1], (tq, nc))
            for mm in range(n_near):
                cbias = jnp.where(m_rel == mm, cb_ref[i][:, mm:mm + 1], cbias)
            z = jnp.where(valid_c, _dot_nt(qh_ref[hf, rows(r), :], kc) + cbias, NEG)
            e = jnp.where(valid_c, jnp.exp(z - jnp.max(z, axis=-1, keepdims=True)), 0.0)
            p = e / jnp.maximum(jnp.sum(e, axis=-1, keepdims=True), 1e-30)
            imp = imp + p
            out_ref[i] = gate(0, i) * _dot(p.astype(BF16), vc)
        imp2 = _dot_exact(imp, pair_ref[...])
        qpos = qi * tq + rowl
        cur = qpos // SEL_BLOCK
        forced = (lane == 0) | (lane == cur) | (lane == cur - 1)
        score = jnp.where(lane * SEL_BLOCK > qpos, NEG, jnp.where(forced, SEL_FORCE, imp2))
        n_blocks = ks_ref.shape[2] // SEL_BLOCK
        rank = jnp.zeros((tq, LANES), F32)
        for j in range(n_blocks):
            sj = score[:, j:j + 1]
            beats = (sj > score) | ((sj == score) & (lane > j))
            rank = rank + jnp.where(beats, 1.0, 0.0)
        sel = jnp.where(rank < N_SEL, 1.0, 0.0).astype(BF16)
        selx_ref[hf] = _dot(sel, e_ref[...])

    rowt = lax.broadcasted_iota(jnp.int32, (tq, tq), 0)
    colt = lax.broadcasted_iota(jnp.int32, (tq, tq), 1)

    def bias_of(i, off):
        return near_ref[i, off] if off < 2 else far_ref[i]

    def reset():
        mx_ref[...] = jnp.full_like(mx_ref, NEG)
        den_ref[...] = jnp.zeros_like(den_ref)
        acc_ref[...] = jnp.zeros_like(acc_ref)

    def score_tile(hf, kt_ref, start, slot, off, keep):
        kt = kt_ref[0, :, pl.ds(start, tq)]
        z = _dot(qh_ref[hf], kt)
        for r in range(rep):
            zr = z[rows(r)] + bias_of(hf * rep + r, off)
            if keep is not None:
                zr = jnp.where(keep, zr, NEG)
            zs_ref[r, :, pl.ds(slot, tq)] = zr
            mx_ref[r] = jnp.maximum(mx_ref[r], zr)

    def fix_max():
        for r in range(rep):
            mx_ref[r] = jnp.broadcast_to(jnp.max(mx_ref[r], axis=-1, keepdims=True), (tq, LANES))

    def value_tile(vt_ref, start, slot):
        vt = vt_ref[0, :, pl.ds(start, tq)]
        ps = []
        for r in range(rep):
            p = jnp.exp(zs_ref[r, :, pl.ds(slot, tq)] - mx_ref[r])
            den_ref[r] += p
            ps.append(p.astype(BF16))
        acc_ref[...] += _dot_nt(jnp.concatenate(ps, axis=0), vt)

    def finish(hf, branch):
        for r in range(rep):
            i = hf * rep + r
            den = jnp.sum(den_ref[r], axis=-1, keepdims=True)
            out_ref[i] += gate(branch, i) * (acc_ref[rows(r), :] / den)

    n_win_tiles = (WINDOW + tq - 1) // tq + 1
    for hf in range(2):
        def keep_sel(start, causal):
            keep = selx_ref[hf, :, pl.ds(start, tq)] > 0.5
            return keep & (colt <= rowt) if causal else keep

        def sel_score(j, off, causal):
            start = pl.multiple_of(j * tq, tq)
            score_tile(hf, ks_ref, start, start, off, keep_sel(start, causal))

        reset()
        sel_score(qi, 0, True)
        pl.when(qi >= 1)(lambda: sel_score(qi - 1, 1, False))

        def far_step(jj, carry):
            sel_score(qi - 2 - jj, 2, False)
            return carry

        lax.fori_loop(0, jnp.maximum(qi - 1, 0), far_step, 0)
        fix_max()

        def sel_value(j, carry):
            start = pl.multiple_of(j * tq, tq)
            value_tile(vs_ref, start, start)
            return carry

        lax.fori_loop(0, qi + 1, sel_value, 0)
        finish(hf, 1)

        def win_score(off):
            dist = off * tq + rowt - colt
            all_valid = off * tq - (tq - 1) >= 0 and off * tq + tq - 1 <= WINDOW
            keep = None if all_valid else (dist >= 0) & (dist <= WINDOW)
            score_tile(hf, kw_ref, pl.multiple_of((qi - off) * tq, tq), off * tq, off, keep)

        def win_value(off):
            value_tile(vw_ref, pl.multiple_of((qi - off) * tq, tq), off * tq)

        reset()
        win_score(0)
        for off in range(1, n_win_tiles):
            pl.when(qi >= off)(functools.partial(win_score, off))
        fix_max()
        win_value(0)
        for off in range(1, n_win_tiles):
            pl.when(qi >= off)(functools.partial(win_value, off))
        finish(hf, 2)

    for c in range(n_heads // 2):
        halves = []
        for i in (2 * c, 2 * c + 1):
            o = out_ref[i]
            if i % 2 != i // rep:
                o = pltpu.roll(o, HEAD_DIM, 1)
            halves.append(o)
        o_ref[:, c * LANES:(c + 1) * LANES] = jnp.where(lo_half, halves[0], halves[1]).astype(BF16)


def _nsa_prompt(q, kvc, selt, wint, gates, rel_bias, batch):
    m, d = q.shape
    t = m // batch
    tq = NSA_TQ
    nq = t // tq
    nc = t // CMP_BLOCK
    nb = t // SEL_BLOCK
    rep = NSA_REP
    n_heads = 2 * rep
    assert t % tq == 0 and nb <= LANES and d == 2 * n_heads * HEAD_DIM and KV_W == 4 * LANES
    assert _bucket_saturates_from(tq + 1, t + WINDOW)
    zs_width = max(t, ((WINDOW + tq - 1) // tq + 1) * tq)

    tl = np.arange(tq)
    near = jnp.stack([_bias_of(rel_bias, o * tq + tl[:, None] - tl[None, :]) for o in range(2)], axis=1)
    far = _bias_of(rel_bias, np.full((1, LANES), 2 * tq))
    shift = tq // CMP_BLOCK - 1
    n_near = shift + -(-(tq + CMP_BLOCK) // CMP_BLOCK) + 1
    assert tq % CMP_BLOCK == 0 and CMP_BLOCK * (n_near - shift) - CMP_BLOCK + 1 > tq
    cb = _bias_of(rel_bias, CMP_BLOCK * (np.arange(n_near + 1)[None, :] - shift) + tl[:, None] - (CMP_BLOCK - 1))
    expand = jnp.asarray(np.arange(LANES)[:, None] == (np.arange(t) // SEL_BLOCK)[None, :], BF16)
    pair = jnp.asarray((np.arange(nc) // (SEL_BLOCK // CMP_BLOCK))[:, None] == np.arange(LANES)[None, :], BF16)

    kvc_spec = lambda col0: pl.BlockSpec((nc, LANES), lambda b, gp, i: (b, col0 + gp))
    kvt_spec = lambda row0: pl.BlockSpec((1, LANES, t), lambda b, gp, i: (b, row0 + gp, 0))
    return pl.pallas_call(
        _nsa_prompt_body, grid=(batch, 2, nq),
        in_specs=[pl.BlockSpec((tq, n_heads * HEAD_DIM), lambda b, gp, i: (b * nq + i, gp)),
                  kvc_spec(0), kvc_spec(2), kvt_spec(0), kvt_spec(2), kvt_spec(0), kvt_spec(2),
                  pl.BlockSpec((tq, LANES), lambda b, gp, i: (b * nq + i, gp)),
                  pl.BlockSpec((n_heads, 2, tq, tq), lambda b, gp, i: (gp, 0, 0, 0)),
                  pl.BlockSpec((n_heads, 1, LANES), lambda b, gp, i: (gp, 0, 0)),
                  pl.BlockSpec((n_heads, tq, n_near + 1), lambda b, gp, i: (gp, 0, 0)),
                  pl.BlockSpec((LANES, t), lambda b, gp, i: (0, 0)),
                  pl.BlockSpec((nc, LANES), lambda b, gp, i: (0, 0))],
        out_specs=pl.BlockSpec((tq, n_heads * HEAD_DIM), lambda b, gp, i: (b * nq + i, gp)),
        out_shape=jax.ShapeDtypeStruct((m, d), BF16),
        scratch_shapes=[pltpu.VMEM((2, rep * tq, LANES), BF16), pltpu.VMEM((2, tq, t), F32),
                        pltpu.VMEM((rep, tq, zs_width), F32), pltpu.VMEM((rep, tq, LANES), F32),
                        pltpu.VMEM((rep, tq, LANES), F32), pltpu.VMEM((rep * tq, LANES), F32),
                        pltpu.VMEM((n_heads, tq, LANES), F32)],
        compiler_params=_params("parallel", "parallel", "arbitrary"), name="nsa_prompt")(
            q, kvc, kvc, selt, selt, wint, wint, gates, near, far, cb, expand, pair)


def _nsa_select_body(q_ref, kvc_ref, cb_ref, pair_ref, oc_ref, idx_ref, *, past, n_take):
    kvc = kvc_ref[...]
    z = _dot_nt(q_ref[0], kvc[:, :HALF_W]) + cb_ref[...]
    e = jnp.exp(z - jnp.max(z, axis=-1, keepdims=True))
    p = e / jnp.maximum(jnp.sum(e, axis=-1, keepdims=True), 1e-30)
    oc = _dot(p.astype(BF16), kvc[:, HALF_W:])
    for g in range(NSA_GROUPS):
        oc_ref[0, g * NSA_REP:(g + 1) * NSA_REP, :] = oc[g * NSA_REP:(g + 1) * NSA_REP,
                                                         g * HEAD_DIM:(g + 1) * HEAD_DIM]
    imp = jnp.concatenate([jnp.sum(p[g * NSA_REP:(g + 1) * NSA_REP], axis=0, keepdims=True)
                           for g in range(NSA_GROUPS)], axis=0)
    imp2 = _dot_exact(imp, pair_ref[...])
    width = imp2.shape[1]
    lane = lax.broadcasted_iota(jnp.int32, (NSA_GROUPS, width), 1)
    lanef = lane.astype(F32)
    cur = past // SEL_BLOCK
    forced = (lane == 0) | (lane == cur) | (lane == cur - 1)
    score = jnp.where(lane * SEL_BLOCK > past, NEG, jnp.where(forced, SEL_FORCE, imp2))
    out_lane = lax.broadcasted_iota(jnp.int32, (NSA_GROUPS, LANES), 1)
    idx = jnp.zeros((NSA_GROUPS, LANES), F32)
    for k in range(n_take):
        best = jnp.max(score, axis=-1, keepdims=True)
        pick = jnp.min(jnp.where(score == best, lanef, float(width)), axis=-1, keepdims=True)
        idx = jnp.where(out_lane == k, pick, idx)
        score = jnp.where(lanef == pick, 2.0 * NEG, score)
    idx_ref[0] = idx.astype(jnp.int32)


def _nsa_select(q16, kvc, cb, past):
    bs, n_heads, _ = q16.shape
    n_cmp = kvc.shape[0] // bs
    n_blocks = past // SEL_BLOCK + 1
    width = -(-n_blocks // LANES) * LANES
    n_take = min(N_SEL, n_blocks)
    pair = jnp.asarray((np.arange(n_cmp) // (SEL_BLOCK // CMP_BLOCK))[:, None] == np.arange(width)[None, :], BF16)
    body = functools.partial(_nsa_select_body, past=past, n_take=n_take)
    return pl.pallas_call(
        body, grid=(bs,),
        in_specs=[pl.BlockSpec((1, n_heads, HALF_W), lambda b: (b, 0, 0)),
                  pl.BlockSpec((n_cmp, KV_W), lambda b: (b, 0)),
                  pl.BlockSpec((n_heads, n_cmp), lambda b: (0, 0)),
                  pl.BlockSpec((n_cmp, width), lambda b: (0, 0))],
        out_specs=[pl.BlockSpec((1, n_heads, HEAD_DIM), lambda b: (b, 0, 0)),
                   pl.BlockSpec((1, NSA_GROUPS, LANES), lambda b: (b, 0, 0))],
        out_shape=[jax.ShapeDtypeStruct((bs, n_heads, HEAD_DIM), F32),
                   jax.ShapeDtypeStruct((bs, NSA_GROUPS, LANES), jnp.int32)],
        compiler_params=_params("parallel"), name="nsa_select")(q16, kvc, cb, pair), n_take


BIAS_TABLE = 2 * LANES


def _lookup_bias(tab, dist):
    n = dist.shape[1]
    onehot = (lax.broadcasted_iota(jnp.int32, (BIAS_TABLE, n), 0) == jnp.minimum(dist, BIAS_TABLE - 1))
    return _dot_exact(tab, jnp.where(onehot, 1.0, 0.0).astype(BF16))


def _nsa_attend_body(pt_ref, ix_ref, q_ref, seln_ref, winn_ref, wcol_ref, win_ref, oc_ref, g_ref, tab_ref, wtab_ref,
                     pick_ref, *rest, past, n_take):
    pages = rest[:n_take]
    o_ref, wout_ref = rest[n_take:]
    b = pl.program_id(0)
    g = pl.program_id(1)
    n_past_blocks = past // SEL_BLOCK
    page = pages[0].shape[2]
    q4 = q_ref[0, 0]
    q4f = q4.astype(F32)
    tab = tab_ref[0]
    gates = g_ref[0, 0]

    zs, kvs = [], []
    has_new = False
    lane = lax.broadcasted_iota(jnp.int32, (1, page), 1)
    for i in range(n_take):
        v = ix_ref[(b * NSA_GROUPS + g) * n_take + i]
        kvt = pages[i][0].astype(BF16)
        kpos = (v * SEL_BLOCK) // page * page + lane
        dist = past - kpos
        ok = (kpos // SEL_BLOCK == v) & (dist >= 0) & (v < n_past_blocks)
        z = _dot(q4, kvt) + _lookup_bias(tab, jnp.maximum(dist, 0))
        zs.append(jnp.where(ok, z, NEG))
        kvs.append(kvt)
        has_new = jnp.logical_or(has_new, v >= n_past_blocks)
    new_s = _bf16_round(seln_ref[0])
    z_new = jnp.sum(q4f * new_s, axis=-1, keepdims=True) + tab[:, :1]
    z_new = jnp.where(has_new, z_new, NEG)
    m = z_new
    for z in zs:
        m = jnp.maximum(m, jnp.max(z, axis=-1, keepdims=True))
    es = [jnp.exp(z - m) for z in zs]
    e_new = jnp.exp(z_new - m)
    den = e_new
    for e in es:
        den = den + jnp.sum(e, axis=-1, keepdims=True)
    den = jnp.maximum(den, 1e-30)
    o_s = _bf16_round(e_new / den) * new_s
    for e, kvt in zip(es, kvs):
        o_s = o_s + _dot_nt((e / den).astype(BF16), kvt)

    w = win_ref[0]
    wb = w.astype(BF16)
    new_w = _bf16_round(winn_ref[0])
    n_win = w.shape[1]
    zw = _dot(q4, wb) + wtab_ref[0][:, :n_win]
    zw_new = jnp.sum(q4f * new_w, axis=-1, keepdims=True) + wtab_ref[0][:, n_win:n_win + 1]
    mw = jnp.maximum(jnp.max(zw, axis=-1, keepdims=True), zw_new)
    ew = jnp.exp(zw - mw)
    ew_new = jnp.exp(zw_new - mw)
    denw = jnp.maximum(jnp.sum(ew, axis=-1, keepdims=True) + ew_new, 1e-30)
    o_w = _dot_nt((ew / denw).astype(BF16), wb) + _bf16_round(ew_new / denw) * new_w

    o_sw = gates[:, 1:2] * o_s + gates[:, 2:3] * o_w
    o_ref[0, 0] = gates[:, 0:1] * oc_ref[0, 0] + _dot_exact(o_sw, pick_ref[0])

    @pl.when(g == 0)
    def _():
        cols = lax.broadcasted_iota(jnp.int32, w.shape, 1)
        wout_ref[0] = jnp.where(cols == n_win - 1, wcol_ref[0], pltpu.roll(w, n_win - 1, 1))


def _nsa_attend(q4, sel_new, win_new, state_win, layer, cache_sel, page_table, idx, n_take, oc, gates, rel_bias):
    bs = q4.shape[0]
    n_pool, page = cache_sel.shape[1:3]
    n_pages = page_table.shape[1]
    past = n_pages * page
    n_past_blocks = past // SEL_BLOCK
    n_win = state_win.shape[2]
    assert _bucket_saturates_from(BIAS_TABLE - 1, past + SEL_BLOCK) and page % SEL_BLOCK == 0
    pages = _position_minor_pages(cache_sel)
    base = layer * n_pool
    win_t = jnp.transpose(state_win, (0, 1, 3, 4, 5, 2)).reshape(state_win.shape[0] * bs, KV_W, n_win)
    tab = _bias_of(rel_bias, np.arange(BIAS_TABLE)).reshape(NSA_GROUPS, NSA_REP, BIAS_TABLE)
    wpad = -(-(n_win + 1) // LANES) * LANES
    wdist = np.maximum(n_win - np.arange(wpad), 0)
    wtab = _bias_of(rel_bias, wdist).reshape(NSA_GROUPS, NSA_REP, wpad)
    feats = np.arange(KV_W)[None, :, None]
    pick = jnp.asarray(feats == HALF_W + np.arange(NSA_GROUPS)[:, None, None] * HEAD_DIM
                       + np.arange(HEAD_DIM)[None, None, :], BF16)

    def page_spec(i):
        def index(b, g, pt, ix):
            v = jnp.minimum(ix[(b * NSA_GROUPS + g) * n_take + i], n_past_blocks - 1)
            return (base + pt[b, (v * SEL_BLOCK) // page], 0, 0)
        return pl.BlockSpec((1, KV_W, page), index)

    grid_spec = pltpu.PrefetchScalarGridSpec(
        num_scalar_prefetch=2, grid=(bs, NSA_GROUPS),
        in_specs=[pl.BlockSpec((1, 1, NSA_REP, KV_W), lambda b, g, pt, ix: (b, g, 0, 0)),
                  pl.BlockSpec((1, 1, KV_W), lambda b, g, pt, ix: (b, 0, 0)),
                  pl.BlockSpec((1, 1, KV_W), lambda b, g, pt, ix: (b, 0, 0)),
                  pl.BlockSpec((1, KV_W, 1), lambda b, g, pt, ix: (b, 0, 0)),
                  pl.BlockSpec((1, KV_W, n_win), lambda b, g, pt, ix: (layer * bs + b, 0, 0)),
                  pl.BlockSpec((1, 1, NSA_REP, HEAD_DIM), lambda b, g, pt, ix: (b, g, 0, 0)),
                  pl.BlockSpec((1, 1, NSA_REP, LANES), lambda b, g, pt, ix: (b, g, 0, 0)),
                  pl.BlockSpec((1, NSA_REP, BIAS_TABLE), lambda b, g, pt, ix: (g, 0, 0)),
                  pl.BlockSpec((1, NSA_REP, wpad), lambda b, g, pt, ix: (g, 0, 0)),
                  pl.BlockSpec((1, KV_W, HEAD_DIM), lambda b, g, pt, ix: (g, 0, 0))]
                 + [page_spec(i) for i in range(n_take)],
        out_specs=[pl.BlockSpec((1, 1, NSA_REP, HEAD_DIM), lambda b, g, pt, ix: (b, g, 0, 0)),
                   pl.BlockSpec((1, KV_W, n_win), lambda b, g, pt, ix: (b, 0, 0))])
    body = functools.partial(_nsa_attend_body, past=past, n_take=n_take)
    return pl.pallas_call(
        body, grid_spec=grid_spec,
        out_shape=[jax.ShapeDtypeStruct((bs, NSA_GROUPS, NSA_REP, HEAD_DIM), F32),
                   jax.ShapeDtypeStruct((bs, KV_W, n_win), F32)],
        compiler_params=_params("parallel", "arbitrary"), name="nsa_attend")(
            page_table, idx, q4, sel_new, win_new, win_new.reshape(bs, KV_W, 1), win_t, oc, gates, tab, wtab, pick,
            *([pages] * n_take))


def _compress_weights(cmp_pos, w_cmp):
    pos = jnp.broadcast_to(jnp.swapaxes(cmp_pos, 0, 1)[:, :, None, :], (CMP_BLOCK, 2, NSA_GROUPS, HEAD_DIM))
    eye = jnp.eye(NSA_GROUPS, dtype=w_cmp.dtype)
    w2 = jnp.einsum('clde,gh->lcgdhe', w_cmp, eye).reshape(CMP_BLOCK, 2, HALF_W, HALF_W)
    return pos.reshape(CMP_BLOCK, KV_W), w2.astype(BF16)


def _nsa_gate_weights(wg, paired):
    d = wg.shape[0]
    if paired:
        w = wg.reshape(d, 3, 2, 2 * NSA_REP).transpose(0, 2, 1, 3).reshape(d, 2, 3 * 2 * NSA_REP)
        w = jnp.pad(w, ((0, 0), (0, 0), (0, LANES - w.shape[2])))
        return w.reshape(d, 2 * LANES).astype(BF16)
    return jnp.pad(wg, ((0, 0), (0, LANES - wg.shape[1]))).astype(BF16)


def _rows_by_position(x_t, lead):
    batch, _, t = x_t.shape
    nd = len(lead)
    return jnp.transpose(x_t.reshape(batch, *lead, t), (0, nd + 1) + tuple(range(1, nd + 1)))


def kernel(x_prompt, x_sample, cache_sb_kv, cache_cmp_kv, cache_sel_kv, state_win_kv, page_table,
           w_in_sb, w_out_sb, w_in_nsa, w_out_nsa, cmp_pos, w_cmp, rel_bias, ln_g, ln_b, w_up, w_down):
    bp, t, d = x_prompt.shape
    bs, tn, _ = x_sample.shape
    assert tn == 1 and d == NSA_GROUPS * NSA_REP * HEAD_DIM
    depth = ln_g.shape[0]
    alpha = (2 * depth) ** 0.25
    past = page_table.shape[1] * cache_sb_kv.shape[2]
    n_heads = d // HEAD_DIM
    sb_lead = (2, n_heads, HEAD_DIM)
    nsa_lead = (2, NSA_GROUPS, HEAD_DIM)
    xp = x_prompt.reshape(bp * t, d)
    xs = x_sample.reshape(bs, d)
    sb_p, sb_s, cmp_p, cmp_s, sel_p, sel_s, win_p, win_s = [], [], [], [], [], [], [], []
    for i in range(depth):
        l = i // 2
        if i % 2 == 0:
            w = w_in_sb[l].astype(BF16)
            wq, wkv = w[:, :d], w[:, d:]
            q, v, kvt, kt = _sb_proj(xp, wq, w[:, 2 * d:], wkv.T, bp)
            mp = _sb_prompt(q, kt, v, bp)
            qs, kvs = _sb_proj_rows(xs, wq, wkv)
            ms = _sb_sample(qs, cache_sb_kv, l, page_table)
            sb_p.append(_rows_by_position(kvt, sb_lead))
            sb_s.append(kvs.reshape(bs, tn, *sb_lead))
            w_out = w_out_sb[l]
        else:
            w = w_in_nsa[l].astype(BF16)
            wq = w[:, :d]
            wkv = w[:, d:d + 3 * KV_W]
            wg = w_in_nsa[l][:, d + 3 * KV_W:]
            pos_rows, w2 = _compress_weights(cmp_pos[l], w_cmp[l])
            q, c_rows, gates, c_t, s_t, w_t, s_tb, w_tb = _nsa_proj(
                xp, wq, wkv[:, :KV_W], _nsa_gate_weights(wg, True), wkv.T, bp)
            kvc = _compress(c_rows.reshape(bp * t // CMP_BLOCK, CMP_BLOCK * KV_W), pos_rows.reshape(1, -1), w2)
            mp = _nsa_prompt(q, kvc, s_tb, w_tb, gates, rel_bias, bp)
            w_buf = min(WINDOW, t)
            cmp_p.append(_rows_by_position(c_t, nsa_lead))
            sel_p.append(_rows_by_position(s_t, nsa_lead))
            win_p.append(_rows_by_position(w_t[:, :, t - w_buf:], nsa_lead))
            qs, kv_s, gates_s = _nsa_proj_rows(xs, wq, wkv, _nsa_gate_weights(wg, False))
            c_s, s_s, w_s = kv_s[:, :KV_W], kv_s[:, KV_W:2 * KV_W], kv_s[:, 2 * KV_W:]
            kvc_s = _compress_paged(cache_cmp_kv, l, page_table, pos_rows, w2)
            qh = qs.reshape(bs, NSA_GROUPS, NSA_REP, 1, HEAD_DIM)
            eye = jnp.eye(NSA_GROUPS, dtype=qs.dtype)[None, :, None, :, None]
            q16 = (qh * eye).reshape(bs, n_heads, HALF_W)
            q4 = jnp.pad(q16, ((0, 0), (0, 0), (0, HALF_W))).reshape(bs, NSA_GROUPS, NSA_REP, KV_W)
            n_cmp = past // CMP_BLOCK
            cb_s = _bias_of(rel_bias, past - (np.arange(n_cmp) * CMP_BLOCK + CMP_BLOCK - 1))
            (oc, idx), n_take = _nsa_select(q16, kvc_s, cb_s, past)
            g3 = gates_s[:, :3 * n_heads].reshape(bs, 3, NSA_GROUPS, NSA_REP).transpose(0, 2, 3, 1)
            g3 = jnp.pad(g3, ((0, 0), (0, 0), (0, 0), (0, LANES - 3)))
            o4, wnew = _nsa_attend(q4, s_s.reshape(bs, 1, KV_W), w_s.reshape(bs, 1, KV_W), state_win_kv, l,
                                   cache_sel_kv, page_table, idx[:, :, :n_take].reshape(-1), n_take,
                                   oc.reshape(bs, NSA_GROUPS, NSA_REP, HEAD_DIM), g3, rel_bias)
            ms = o4.reshape(bs, d)
            cmp_s.append(c_s.reshape(bs, tn, *nsa_lead))
            sel_s.append(s_s.reshape(bs, tn, *nsa_lead))
            win_s.append(_rows_by_position(wnew, nsa_lead))
            w_out = w_out_nsa[l]
        tail_args = (w_out.astype(BF16), ln_g[i, 0][None], ln_b[i, 0][None], w_up[i].astype(BF16),
                     w_down[i].astype(BF16), ln_g[i, 1][None], ln_b[i, 1][None], alpha)
        xp = _tail(xp, mp, *tail_args)
        xs = _tail(xs, ms, *tail_args)
    return (xp.reshape(bp, t, d), xs.reshape(bs, tn, d), jnp.stack(sb_p), jnp.stack(sb_s), jnp.stack(cmp_p),
            jnp.stack(cmp_s), jnp.stack(sel_p), jnp.stack(sel_s), jnp.stack(win_p), jnp.stack(win_s))
```

```python
import functools
import math

import numpy as np
import jax
import jax.numpy as jnp
from jax import lax
from jax.experimental import pallas as pl
from jax.experimental.pallas import tpu as pltpu

F32 = jnp.float32
BF16 = jnp.bfloat16

HEAD_DIM = 64
NSA_GROUPS = 4
NSA_REP = 4
CMP_BLOCK = 32
SEL_BLOCK = 64
N_SEL = 16
WINDOW = 512
N_BUCKETS = 32
MAX_DISTANCE = 128
LN_EPS = 1e-5
SEL_FORCE = 1e4
NEG = -1e30

LANES = 128
ROW_TILE = 256
SB_TILE = 256
NSA_TQ = 256
SB_PAGES_PER_STEP = 8
CMP_PAGES_PER_STEP = 8
VMEM_LIMIT = 56 * 1024 * 1024

KV_W = 2 * NSA_GROUPS * HEAD_DIM
HALF_W = NSA_GROUPS * HEAD_DIM

_NT = (((1,), (1,)), ((), ()))


def _dot(a, b):
    return jnp.dot(a, b, preferred_element_type=F32)


def _dot_nt(a, b):
    return lax.dot_general(a, b, _NT, preferred_element_type=F32)


def _split2(x):
    hi = x.astype(BF16)
    lo = (x - hi.astype(F32)).astype(BF16)
    return hi, lo


def _split3(x):
    hi = x.astype(BF16)
    r = x - hi.astype(F32)
    mid = r.astype(BF16)
    lo = (r - mid.astype(F32)).astype(BF16)
    return hi, mid, lo


def _dot_exact(x, m):
    hi, mid, lo = _split3(x)
    return _dot(hi, m) + _dot(mid, m) + _dot(lo, m)


def _softplus(z):
    return jnp.maximum(z, 0.0) + jnp.log1p(jnp.exp(-jnp.abs(z)))


def _layer_norm(y, g, b):
    mu = jnp.mean(y, axis=-1, keepdims=True)
    yc = y - mu
    var = jnp.mean(yc * yc, axis=-1, keepdims=True)
    return yc * lax.rsqrt(var + LN_EPS) * g + b


def _bf16_round(x):
    return x.astype(BF16).astype(F32)


def _params(*sem):
    return pltpu.CompilerParams(dimension_semantics=sem, vmem_limit_bytes=VMEM_LIMIT)


def _const_spec(a):
    nd = a.ndim
    return pl.BlockSpec(a.shape, lambda *_: (0,) * nd)


def _row_call(body, row_args, const_args, outs, name, t_outs=(), seq_len=None):
    m = row_args[0].shape[0]
    tm = min(ROW_TILE, m)
    assert m % tm == 0
    in_specs = [pl.BlockSpec((tm, a.shape[1]), lambda i: (i, 0)) for a in row_args]
    in_specs += [_const_spec(a) for a in const_args]
    out_specs = [pl.BlockSpec((tm, o.shape[1]), lambda i: (i, 0)) for o in outs]
    if t_outs:
        nt = seq_len // tm
        assert seq_len % tm == 0
        out_specs += [pl.BlockSpec((1, o.shape[1], tm), lambda i: (i // nt, 0, i % nt)) for o in t_outs]
    return pl.pallas_call(
        body, grid=(m // tm,), in_specs=in_specs, out_specs=out_specs, out_shape=list(outs) + list(t_outs),
        compiler_params=_params("parallel"), name=name)(*row_args, *const_args)


def _sb_proj_rows_body(x_ref, wq_ref, wkv_ref, q_ref, kv_ref):
    xb = x_ref[...].astype(BF16)
    q_ref[...] = (_dot(xb, wq_ref[...]) * HEAD_DIM ** -0.5).astype(BF16)
    kv_ref[...] = _dot(xb, wkv_ref[...])


def _sb_proj_rows(x, wq, wkv):
    m, d = x.shape
    outs = [jax.ShapeDtypeStruct((m, d), BF16), jax.ShapeDtypeStruct((m, 2 * d), F32)]
    return _row_call(_sb_proj_rows_body, [x], [wq, wkv], outs, "sb_proj_rows")


def _sb_proj_body(x_ref, wq_ref, wv_ref, wkvt_ref, q_ref, v_ref, kvt_ref, kt_ref):
    xb = x_ref[...].astype(BF16)
    q_ref[...] = (_dot(xb, wq_ref[...]) * HEAD_DIM ** -0.5).astype(BF16)
    v_ref[...] = _dot(xb, wv_ref[...]).astype(BF16)
    kvt = _dot_nt(wkvt_ref[...], xb)
    kvt_ref[0] = kvt
    kt_ref[0] = kvt[:kt_ref.shape[1]].astype(BF16)


def _sb_proj(x, wq, wv, wkvt, batch):
    m, d = x.shape
    t = m // batch
    outs = [jax.ShapeDtypeStruct((m, d), BF16), jax.ShapeDtypeStruct((m, d), BF16)]
    t_outs = [jax.ShapeDtypeStruct((batch, 2 * d, t), F32), jax.ShapeDtypeStruct((batch, d, t), BF16)]
    return _row_call(_sb_proj_body, [x], [wq, wv, wkvt], outs, "sb_proj", t_outs, t)


def _tail_body(x_ref, m_ref, wo_ref, g1_ref, b1_ref, wu_ref, wd_ref, g2_ref, b2_ref, o_ref, *, alpha):
    y = alpha * x_ref[...] + _dot(m_ref[...].astype(BF16), wo_ref[...])
    y = _layer_norm(y, g1_ref[...], b1_ref[...])
    h = jnp.maximum(_dot(y.astype(BF16), wu_ref[...]), 0.0)
    y = alpha * y + _dot((h * h).astype(BF16), wd_ref[...])
    o_ref[...] = _layer_norm(y, g2_ref[...], b2_ref[...])


def _tail(x, mix, wo, g1, b1, wu, wd, g2, b2, alpha):
    outs = [jax.ShapeDtypeStruct(x.shape, F32)]
    body = functools.partial(_tail_body, alpha=alpha)
    return _row_call(body, [x, mix], [wo, g1, b1, wu, wd, g2, b2], outs, "block_tail")[0]


def _rev_cumsum_matrix(n):
    j = np.arange(n)
    return jnp.asarray(j[:, None] >= j[None, :], BF16)


def _sb_local(z, u, mask):
    sp = _softplus(z)
    if mask is not None:
        sp = jnp.where(mask, sp, 0.0)
    hi, lo = _split2(sp)
    cum = _dot(hi, u) + _dot(lo, u)
    return z - cum, cum[:, :1]


def _sb_weights(d, suf, mask):
    a = jnp.exp(d - suf)
    if mask is not None:
        a = jnp.where(mask, a, 0.0)
    return a.astype(BF16)


def _sb_prompt_body(q_ref, kt_ref, v_ref, u_ref, o_ref, acc_ref):
    tq = q_ref.shape[0]
    qi = pl.program_id(2)
    lane = lax.broadcasted_iota(jnp.int32, (tq, LANES), 1)
    lo_half = lane < HEAD_DIM
    q2 = q_ref[...]
    zero = jnp.zeros_like(q2)
    qs = (jnp.where(lo_half, q2, zero), jnp.where(lo_half, zero, q2))
    u = u_ref[...]
    row = lax.broadcasted_iota(jnp.int32, (tq, tq), 0)
    col = lax.broadcasted_iota(jnp.int32, (tq, tq), 1)
    strictly_before = col < row
    acc_ref[...] = jnp.zeros_like(acc_ref)

    def tiles(js, sufs, mask):
        local = []
        for j in js:
            start = pl.multiple_of(j * tq, tq)
            kt2 = kt_ref[0, :, pl.ds(start, tq)]
            v2 = v_ref[pl.ds(start, tq), :]
            local.append((v2, [_sb_local(_dot(qs[h], kt2), u, mask) for h in range(2)]))
        sufs = list(sufs)
        for v2, per_head in local:
            for h, (d, tot) in enumerate(per_head):
                acc_ref[h] += _dot(_sb_weights(d, sufs[h], mask), v2)
                sufs[h] = sufs[h] + tot
        return tuple(sufs)

    zero_suf = jnp.zeros((tq, 1), F32)
    sufs = tiles([qi], (zero_suf, zero_suf), strictly_before)
    sufs = lax.fori_loop(0, qi // 2, lambda jj, s: tiles([qi - 1 - 2 * jj, qi - 2 - 2 * jj], s, None), sufs)

    @pl.when(qi % 2 == 1)
    def _():
        tiles([0], sufs, None)

    o_ref[...] = jnp.where(lo_half, acc_ref[0], acc_ref[1]).astype(BF16)


def _sb_prompt(q, kt, v, batch):
    m, d = q.shape
    t = m // batch
    tq = min(SB_TILE, t)
    assert t % tq == 0 and d % LANES == 0
    nq = t // tq
    u = _rev_cumsum_matrix(tq)
    return pl.pallas_call(
        _sb_prompt_body, grid=(batch, d // LANES, nq),
        in_specs=[pl.BlockSpec((tq, LANES), lambda b, h, i: (b * nq + i, h)),
                  pl.BlockSpec((1, LANES, t), lambda b, h, i: (b, h, 0)),
                  pl.BlockSpec((t, LANES), lambda b, h, i: (b, h)),
                  pl.BlockSpec((tq, tq), lambda b, h, i: (0, 0))],
        out_specs=pl.BlockSpec((tq, LANES), lambda b, h, i: (b * nq + i, h)),
        out_shape=jax.ShapeDtypeStruct((m, d), BF16),
        scratch_shapes=[pltpu.VMEM((2, tq, LANES), F32)],
        compiler_params=_params("parallel", "parallel", "arbitrary"), name="sb_prompt")(q, kt, v, u)


def _position_minor_pages(cache):
    n_layers, n_pool, page = cache.shape[:3]
    feat = cache.shape[3] * cache.shape[4] * cache.shape[5]
    return jnp.transpose(cache, (0, 1, 3, 4, 5, 2)).reshape(n_layers * n_pool, feat, page)


def _sb_sample_body(pt_ref, q_ref, u_ref, *rest, pages_per_step):
    pages = rest[:pages_per_step]
    o_ref, acc_ref, suf_ref = rest[pages_per_step:]
    s = pl.program_id(1)
    d = q_ref.shape[-1]
    heads = d // HEAD_DIM

    @pl.when(s == 0)
    def _():
        acc_ref[...] = jnp.zeros_like(acc_ref)
        suf_ref[...] = jnp.zeros_like(suf_ref)

    own = (lax.broadcasted_iota(jnp.int32, (heads, d), 1) // HEAD_DIM
           == lax.broadcasted_iota(jnp.int32, (heads, d), 0))
    qrow = jnp.broadcast_to(q_ref[0].astype(F32), (heads, d))
    qbd = jnp.where(own, qrow, 0.0).astype(BF16)
    u = u_ref[...]
    acc = acc_ref[...]
    suf = suf_ref[...]
    local = []
    for p in range(pages_per_step):
        kt = pages[p][0, :d, :].astype(BF16)
        local.append(_sb_local(_dot(qbd, kt), u, None))
    for p, (dl, tot) in enumerate(local):
        vt = pages[p][0, d:, :].astype(BF16)
        acc = acc + _dot_nt(_sb_weights(dl, suf, None), vt)
        suf = suf + tot
    acc_ref[...] = acc
    suf_ref[...] = suf

    @pl.when(s == pl.num_programs(1) - 1)
    def _():
        o_ref[0] = jnp.sum(jnp.where(own, acc, 0.0), axis=0, keepdims=True)


def _sb_sample(q, cache, layer, page_table):
    b, d = q.shape
    n_pool, page = cache.shape[1:3]
    assert cache.shape[3] * cache.shape[4] * cache.shape[5] == 2 * d
    n_pages = page_table.shape[1]
    pps = math.gcd(SB_PAGES_PER_STEP, n_pages)
    pages = _position_minor_pages(cache)
    base = layer * n_pool

    def page_spec(p):
        return pl.BlockSpec((1, 2 * d, page),
                            lambda i, s, pt: (base + pt[i, n_pages - 1 - (s * pps + p)], 0, 0))

    grid_spec = pltpu.PrefetchScalarGridSpec(
        num_scalar_prefetch=1, grid=(b, n_pages // pps),
        in_specs=[pl.BlockSpec((1, 1, d), lambda i, s, pt: (i, 0, 0)),
                  pl.BlockSpec((page, page), lambda i, s, pt: (0, 0))]
                 + [page_spec(p) for p in range(pps)],
        out_specs=pl.BlockSpec((1, 1, d), lambda i, s, pt: (i, 0, 0)),
        scratch_shapes=[pltpu.VMEM((d // HEAD_DIM, d), F32), pltpu.VMEM((d // HEAD_DIM, 1), F32)])
    out = pl.pallas_call(
        functools.partial(_sb_sample_body, pages_per_step=pps), grid_spec=grid_spec,
        out_shape=jax.ShapeDtypeStruct((b, 1, d), F32),
        compiler_params=_params("parallel", "arbitrary"), name="sb_sample")(
            page_table, q.reshape(b, 1, d), _rev_cumsum_matrix(page), *([pages] * pps))
    return out.reshape(b, d)


def _nsa_proj_rows_body(x_ref, wq_ref, wkv_ref, wg_ref, q_ref, kv_ref, g_ref):
    xb = x_ref[...].astype(BF16)
    q_ref[...] = (_dot(xb, wq_ref[...]) * HEAD_DIM ** -0.5).astype(BF16)
    kv_ref[...] = _dot(xb, wkv_ref[...])
    g_ref[...] = jax.nn.sigmoid(_dot(xb, wg_ref[...]))


def _nsa_proj_rows(x, wq, wkv, wg):
    m, d = x.shape
    outs = [jax.ShapeDtypeStruct((m, d), BF16), jax.ShapeDtypeStruct((m, wkv.shape[1]), F32),
            jax.ShapeDtypeStruct((m, wg.shape[1]), F32)]
    return _row_call(_nsa_proj_rows_body, [x], [wq, wkv, wg], outs, "nsa_proj_rows")


def _nsa_proj_body(x_ref, wq_ref, wc_ref, wg_ref, wkvt_ref, q_ref, cmp_ref, g_ref,
                   cmpt_ref, selt_ref, wint_ref, seltb_ref, wintb_ref):
    xb = x_ref[...].astype(BF16)
    q_ref[...] = (_dot(xb, wq_ref[...]) * HEAD_DIM ** -0.5).astype(BF16)
    cmp_ref[...] = _dot(xb, wc_ref[...])
    g_ref[...] = jax.nn.sigmoid(_dot(xb, wg_ref[...]))
    kvt = _dot_nt(wkvt_ref[...], xb)
    cmpt_ref[0] = kvt[:KV_W]
    sel = kvt[KV_W:2 * KV_W]
    win = kvt[2 * KV_W:]
    selt_ref[0] = sel
    wint_ref[0] = win
    seltb_ref[0] = sel.astype(BF16)
    wintb_ref[0] = win.astype(BF16)


def _nsa_proj(x, wq, wc, wg, wkvt, batch):
    m, d = x.shape
    t = m // batch
    outs = [jax.ShapeDtypeStruct((m, d), BF16), jax.ShapeDtypeStruct((m, KV_W), F32),
            jax.ShapeDtypeStruct((m, wg.shape[1]), F32)]
    t_outs = [jax.ShapeDtypeStruct((batch, KV_W, t), F32)] * 3 + [jax.ShapeDtypeStruct((batch, KV_W, t), BF16)] * 2
    return _row_call(_nsa_proj_body, [x], [wq, wc, wg, wkvt], outs, "nsa_proj", t_outs, t)


CMP_L_CHUNK = 4


def _compress_body(x_ref, pos_ref, w_ref, o_ref, acc_ref):
    kc = pl.program_id(1)

    @pl.when(kc == 0)
    def _():
        acc_ref[...] = jnp.zeros_like(acc_ref)

    xb = (x_ref[...] + pos_ref[...]).astype(BF16)
    for l in range(CMP_L_CHUNK):
        for c in range(2):
            lo = l * KV_W + c * HALF_W
            acc_ref[:, c * HALF_W:(c + 1) * HALF_W] += _dot(xb[:, lo:lo + HALF_W], w_ref[l, c])

    @pl.when(kc == pl.num_programs(1) - 1)
    def _():
        o_ref[...] = acc_ref[...].astype(BF16)


def _compress(x2, pos_flat, w2):
    rows, width = x2.shape
    tm = min(512, rows)
    assert rows % tm == 0 and CMP_BLOCK % CMP_L_CHUNK == 0
    kw = CMP_L_CHUNK * KV_W
    return pl.pallas_call(
        _compress_body, grid=(rows // tm, width // kw),
        in_specs=[pl.BlockSpec((tm, kw), lambda i, k: (i, k)),
                  pl.BlockSpec((1, kw), lambda i, k: (0, k)),
                  pl.BlockSpec((CMP_L_CHUNK, 2, HALF_W, HALF_W), lambda i, k: (k, 0, 0, 0))],
        out_specs=pl.BlockSpec((tm, KV_W), lambda i, k: (i, 0)),
        out_shape=jax.ShapeDtypeStruct((rows, KV_W), BF16),
        scratch_shapes=[pltpu.VMEM((tm, KV_W), F32)],
        compiler_params=_params("parallel", "arbitrary"), name="nsa_compress")(x2, pos_flat, w2)


def _compress_paged_body(pt_ref, pos_ref, perm_ref, w_ref, *rest, pages_per_step, steps_per_dot):
    pages = rest[:pages_per_step]
    o_ref, x_ref = rest[pages_per_step:]
    s = pl.program_id(1)
    slot = s % steps_per_dot
    pos = pos_ref[...]
    group = perm_ref.shape[0] // pages[0].shape[2]
    slab = perm_ref.shape[0] // CMP_BLOCK
    r0 = pl.multiple_of(slot * (pages_per_step // group) * slab, slab)
    for pg in range(pages_per_step // group):
        a = jnp.concatenate([pages[pg * group + k][0] + pos for k in range(group)], axis=1).astype(BF16)
        xp = _dot_nt(perm_ref[...], a)
        for l in range(CMP_BLOCK):
            x_ref[l, pl.ds(r0 + pg * slab, slab), :] = xp[l * slab:(l + 1) * slab, :]

    @pl.when(slot == steps_per_dot - 1)
    def _():
        acc = [None, None]
        for l in range(CMP_BLOCK):
            xl = x_ref[l].astype(BF16)
            for c in range(2):
                part = _dot(xl[:, c * HALF_W:(c + 1) * HALF_W], w_ref[l, c])
                acc[c] = part if acc[c] is None else acc[c] + part
        o_ref[...] = jnp.concatenate(acc, axis=1).astype(BF16)


def _compress_paged(cache, layer, page_table, pos_rows, w2):
    n_pool, page = cache.shape[1:3]
    bs, n_pages = page_table.shape
    rpp = page // CMP_BLOCK
    pages = _position_minor_pages(cache)
    base = layer * n_pool
    pps = math.gcd(CMP_PAGES_PER_STEP, n_pages)
    spd = math.gcd(8, n_pages // pps)
    rows = pps * spd * rpp
    n_steps = n_pages // pps
    group = 8 // rpp
    assert 8 % rpp == 0 and pps % group == 0
    pos_t = jnp.tile(pos_rows, (rpp, 1)).T
    l_, k_, n_ = np.meshgrid(np.arange(CMP_BLOCK), np.arange(group), np.arange(rpp), indexing="ij")
    perm = np.zeros((group * page, group * page), np.float32)
    perm[(l_ * group * rpp + k_ * rpp + n_).ravel(), (k_ * page + n_ * CMP_BLOCK + l_).ravel()] = 1.0

    def page_spec(p):
        return pl.BlockSpec((1, KV_W, page), lambda i, s, pt: (base + pt[i, s * pps + p], 0, 0))

    grid_spec = pltpu.PrefetchScalarGridSpec(
        num_scalar_prefetch=1, grid=(bs, n_steps),
        in_specs=[pl.BlockSpec((KV_W, page), lambda i, s, pt: (0, 0)),
                  pl.BlockSpec(perm.shape, lambda i, s, pt: (0, 0)),
                  pl.BlockSpec(w2.shape, lambda i, s, pt: (0, 0, 0, 0))]
                 + [page_spec(p) for p in range(pps)],
        out_specs=pl.BlockSpec((rows, KV_W), lambda i, s, pt: (i * (n_steps // spd) + s // spd, 0)),
        scratch_shapes=[pltpu.VMEM((CMP_BLOCK, rows, KV_W), F32)])
    body = functools.partial(_compress_paged_body, pages_per_step=pps, steps_per_dot=spd)
    return pl.pallas_call(
        body, grid_spec=grid_spec, out_shape=jax.ShapeDtypeStruct((bs * n_pages * rpp, KV_W), BF16),
        compiler_params=_params("parallel", "arbitrary"), name="nsa_compress_paged")(
            page_table, pos_t, jnp.asarray(perm, BF16), w2, *([pages] * pps))


def _rel_bucket(dist):
    dist = jnp.maximum(dist, 0)
    max_exact = N_BUCKETS // 2
    dd = jnp.maximum(dist, 1).astype(F32)
    large = max_exact + (jnp.log(dd / max_exact) / math.log(MAX_DISTANCE / max_exact)
                         * (N_BUCKETS - max_exact)).astype(jnp.int32)
    large = jnp.minimum(large, N_BUCKETS - 1)
    return jnp.where(dist < max_exact, dist, large)


def _bias_of(rel_bias, dist):
    onehot = jax.nn.one_hot(_rel_bucket(jnp.asarray(dist, jnp.int32)), N_BUCKETS, dtype=F32)
    return jnp.einsum('...k,kh->h...', onehot, rel_bias.astype(F32), precision=lax.Precision.HIGHEST)


def _bucket_saturates_from(d0, d1):
    dd = np.arange(d0, d1 + 1).astype(np.float32)
    max_exact = N_BUCKETS // 2
    large = max_exact + (np.log(dd / np.float32(max_exact)) / np.float32(math.log(MAX_DISTANCE / max_exact))
                         * np.float32(N_BUCKETS - max_exact)).astype(np.int32)
    return bool(np.all(large >= N_BUCKETS - 1)) and d0 >= max_exact


def _nsa_prompt_body(q_ref, kc_ref, vc_ref, ks_ref, vs_ref, kw_ref, vw_ref, g_ref, near_ref, far_ref, edge_ref,
                     cb_ref, e_ref, pair_ref, o_ref, qh_ref, selx_ref, zs_ref, mx_ref, den_ref, acc_ref, out_ref):
    tq = q_ref.shape[0]
    nc = kc_ref.shape[0]
    rep = NSA_REP
    n_heads = 2 * rep
    qi = pl.program_id(2)
    lane = lax.broadcasted_iota(jnp.int32, (tq, LANES), 1)
    rowl = lax.broadcasted_iota(jnp.int32, (tq, LANES), 0)
    lo_half = lane < HEAD_DIM
    half_mask = (lo_half, jnp.logical_not(lo_half))
    gates = g_ref[...]

    def gate(branch, i):
        k = branch * n_heads + i
        return gates[:, k:k + 1]

    def rows(r):
        return slice(r * tq, (r + 1) * tq)

    for i in range(n_heads):
        hf, r = divmod(i, rep)
        qf = q_ref[:, (i // 2) * LANES:(i // 2 + 1) * LANES].astype(F32)
        if i % 2 != hf:
            qf = pltpu.roll(qf, HEAD_DIM, 1)
        qh_ref[hf, rows(r), :] = jnp.where(half_mask[hf], qf, 0.0).astype(BF16)

    coln = lax.broadcasted_iota(jnp.int32, (tq, nc), 1)
    rown = lax.broadcasted_iota(jnp.int32, (tq, nc), 0)
    valid_c = qi * tq + rown - (coln * CMP_BLOCK + CMP_BLOCK - 1) >= 0
    m_rel = (qi + 1) * (tq // CMP_BLOCK) - 1 - coln
    n_near = cb_ref.shape[2] - 1
    kc = kc_ref[...]
    vc = vc_ref[...]
    n_blocks = ks_ref.shape[2] // SEL_BLOCK
    nb8 = -(-n_blocks // 8) * 8
    blk_t = lax.broadcasted_iota(jnp.int32, (nb8, tq), 0)
    for hf in range(2):
        imp = jnp.zeros((tq, nc), F32)
        for r in range(rep):
            i = hf * rep + r
            cbias = jnp.broadcast_to(cb_ref[i][:, n_near:n_near + 1], (tq, nc))
            for mm in range(n_near):
                cbias = jnp.where(m_rel == mm, cb_ref[i][:, mm:mm + 1], cbias)
            z = jnp.where(valid_c, _dot_nt(qh_ref[hf, rows(r), :], kc) + cbias, NEG)
            e = jnp.where(valid_c, jnp.exp(z - jnp.max(z, axis=-1, keepdims=True)), 0.0)
            p = e / jnp.maximum(jnp.sum(e, axis=-1, keepdims=True), 1e-30)
            imp = imp + p
            out_ref[i] = gate(0, i) * _dot(p.astype(BF16), vc)
        imp2 = _dot_exact(imp, pair_ref[...])
        qpos = qi * tq + rowl
        cur = qpos // SEL_BLOCK
        forced = (lane == 0) | (lane == cur) | (lane == cur - 1)
        score = jnp.where(lane * SEL_BLOCK > qpos, NEG, jnp.where(forced, SEL_FORCE, imp2))
        st = score.T[:nb8]
        rank = jnp.zeros((nb8, tq), F32)
        for j in range(n_blocks):
            sj = st[j:j + 1, :]
            beats = (sj > st) | ((sj == st) & (blk_t > j))
            rank = rank + jnp.where(beats, 1.0, 0.0)
        sel_t = jnp.where(rank < N_SEL, 1.0, 0.0)
        sel = jnp.concatenate([sel_t, jnp.zeros((LANES - nb8, tq), F32)], axis=0).T.astype(BF16)
        selx_ref[hf] = (_dot(sel, e_ref[...]) - 1.0) * (-NEG)

    n_win_tiles = WINDOW // tq + 1

    def bias_of(i, off, window):
        if off < 2:
            return near_ref[i, off]
        if window and off == n_win_tiles - 1:
            return far_ref[i] + edge_ref[...]
        return far_ref[i]

    def reset():
        mx_ref[...] = jnp.full_like(mx_ref, NEG)
        den_ref[...] = jnp.zeros_like(den_ref)
        acc_ref[...] = jnp.zeros_like(acc_ref)

    def score_tile(hf, kt_ref, start, slot, off, window):
        kt = kt_ref[0, :, pl.ds(start, tq)]
        z = _dot(qh_ref[hf], kt)
        for r in range(rep):
            zr = z[rows(r)] + bias_of(hf * rep + r, off, window)
            if not window:
                zr = zr + selx_ref[hf, :, pl.ds(start, tq)]
            zs_ref[r, :, pl.ds(slot, tq)] = zr
            mx_ref[r] = jnp.maximum(mx_ref[r], zr)

    def fix_max():
        for r in range(rep):
            mx_ref[r] = jnp.broadcast_to(jnp.max(mx_ref[r], axis=-1, keepdims=True), (tq, tq))

    def value_tile(vt_ref, start, slot):
        vt = vt_ref[0, :, pl.ds(start, tq)]
        ps = []
        for r in range(rep):
            p = jnp.exp(zs_ref[r, :, pl.ds(slot, tq)] - mx_ref[r])
            den_ref[r] += p
            ps.append(p.astype(BF16))
        acc_ref[...] += _dot_nt(jnp.concatenate(ps, axis=0), vt)

    def finish(hf, branch):
        for r in range(rep):
            i = hf * rep + r
            den = jnp.sum(den_ref[r], axis=-1, keepdims=True)
            out_ref[i] += gate(branch, i) * (acc_ref[rows(r), :] / den)

    for hf in range(2):
        def sel_score(j, off):
            start = pl.multiple_of(j * tq, tq)
            score_tile(hf, ks_ref, start, start, off, False)

        reset()
        sel_score(qi, 0)
        pl.when(qi >= 1)(lambda: sel_score(qi - 1, 1))

        def far_step(jj, carry):
            sel_score(qi - 2 - jj, 2)
            return carry

        lax.fori_loop(0, jnp.maximum(qi - 1, 0), far_step, 0)
        fix_max()

        def sel_value(j, carry):
            start = pl.multiple_of(j * tq, tq)
            value_tile(vs_ref, start, start)
            return carry

        lax.fori_loop(0, qi + 1, sel_value, 0)
        finish(hf, 1)

        def win_score(off):
            score_tile(hf, kw_ref, pl.multiple_of((qi - off) * tq, tq), off * tq, off, True)

        def win_value(off):
            value_tile(vw_ref, pl.multiple_of((qi - off) * tq, tq), off * tq)

        reset()
        win_score(0)
        for off in range(1, n_win_tiles):
            pl.when(qi >= off)(functools.partial(win_score, off))
        fix_max()
        win_value(0)
        for off in range(1, n_win_tiles):
            pl.when(qi >= off)(functools.partial(win_value, off))
        finish(hf, 2)

    for c in range(n_heads // 2):
        halves = []
        for i in (2 * c, 2 * c + 1):
            o = out_ref[i]
            if i % 2 != i // rep:
                o = pltpu.roll(o, HEAD_DIM, 1)
            halves.append(o)
        o_ref[:, c * LANES:(c + 1) * LANES] = jnp.where(lo_half, halves[0], halves[1]).astype(BF16)


def _nsa_prompt(q, kvc, selt, wint, gates, rel_bias, batch):
    m, d = q.shape
    t = m // batch
    tq = min(NSA_TQ, t)
    nq = t // tq
    nc = t // CMP_BLOCK
    nb = t // SEL_BLOCK
    rep = NSA_REP
    n_heads = 2 * rep
    assert t % tq == 0 and nb <= LANES and d == 2 * n_heads * HEAD_DIM and KV_W == 4 * LANES
    assert WINDOW % tq == 0 and WINDOW // tq >= 2 and tq % LANES == 0
    assert _bucket_saturates_from(tq + 1, t + WINDOW)
    n_win_tiles = WINDOW // tq + 1
    zs_width = max(t, n_win_tiles * tq)

    tl = np.arange(tq)
    dist0 = tl[:, None] - tl[None, :]
    near = jnp.stack([_bias_of(rel_bias, dist0) + jnp.where(dist0 >= 0, 0.0, NEG),
                      _bias_of(rel_bias, tq + dist0)], axis=1)
    far = _bias_of(rel_bias, np.full((1, tq), 2 * tq))
    edge = jnp.asarray(np.where((n_win_tiles - 1) * tq + dist0 <= WINDOW, 0.0, NEG), F32)
    shift = tq // CMP_BLOCK - 1
    n_near = shift + -(-(tq + CMP_BLOCK) // CMP_BLOCK) + 1
    assert tq % CMP_BLOCK == 0 and CMP_BLOCK * (n_near - shift) - CMP_BLOCK + 1 > tq
    cb = _bias_of(rel_bias, CMP_BLOCK * (np.arange(n_near + 1)[None, :] - shift) + tl[:, None] - (CMP_BLOCK - 1))
    expand = jnp.asarray(np.arange(LANES)[:, None] == (np.arange(t) // SEL_BLOCK)[None, :], BF16)
    pair = jnp.asarray((np.arange(nc) // (SEL_BLOCK // CMP_BLOCK))[:, None] == np.arange(LANES)[None, :], BF16)

    kvc_spec = lambda col0: pl.BlockSpec((nc, LANES), lambda b, gp, i: (b, col0 + gp))
    kvt_spec = lambda row0: pl.BlockSpec((1, LANES, t), lambda b, gp, i: (b, row0 + gp, 0))
    return pl.pallas_call(
        _nsa_prompt_body, grid=(batch, 2, nq),
        in_specs=[pl.BlockSpec((tq, n_heads * HEAD_DIM), lambda b, gp, i: (b * nq + i, gp)),
                  kvc_spec(0), kvc_spec(2), kvt_spec(0), kvt_spec(2), kvt_spec(0), kvt_spec(2),
                  pl.BlockSpec((tq, LANES), lambda b, gp, i: (b * nq + i, gp)),
                  pl.BlockSpec((n_heads, 2, tq, tq), lambda b, gp, i: (gp, 0, 0, 0)),
                  pl.BlockSpec((n_heads, 1, tq), lambda b, gp, i: (gp, 0, 0)),
                  pl.BlockSpec((tq, tq), lambda b, gp, i: (0, 0)),
                  pl.BlockSpec((n_heads, tq, n_near + 1), lambda b, gp, i: (gp, 0, 0)),
                  pl.BlockSpec((LANES, t), lambda b, gp, i: (0, 0)),
                  pl.BlockSpec((nc, LANES), lambda b, gp, i: (0, 0))],
        out_specs=pl.BlockSpec((tq, n_heads * HEAD_DIM), lambda b, gp, i: (b * nq + i, gp)),
        out_shape=jax.ShapeDtypeStruct((m, d), BF16),
        scratch_shapes=[pltpu.VMEM((2, rep * tq, LANES), BF16), pltpu.VMEM((2, tq, t), F32),
                        pltpu.VMEM((rep, tq, zs_width), F32), pltpu.VMEM((rep, tq, tq), F32),
                        pltpu.VMEM((rep, tq, tq), F32), pltpu.VMEM((rep * tq, LANES), F32),
                        pltpu.VMEM((n_heads, tq, LANES), F32)],
        compiler_params=_params("parallel", "parallel", "arbitrary"), name="nsa_prompt")(
            q, kvc, kvc, selt, selt, wint, wint, gates, near, far, edge, cb, expand, pair)


def _nsa_select_body(q_ref, kvc_ref, cb_ref, pair_ref, oc_ref, idx_ref, *, past, n_take):
    kvc = kvc_ref[...]
    z = _dot_nt(q_ref[0], kvc[:, :HALF_W]) + cb_ref[...]
    e = jnp.exp(z - jnp.max(z, axis=-1, keepdims=True))
    p = e / jnp.maximum(jnp.sum(e, axis=-1, keepdims=True), 1e-30)
    oc = _dot(p.astype(BF16), kvc[:, HALF_W:])
    for g in range(NSA_GROUPS):
        oc_ref[0, g * NSA_REP:(g + 1) * NSA_REP, :] = oc[g * NSA_REP:(g + 1) * NSA_REP,
                                                         g * HEAD_DIM:(g + 1) * HEAD_DIM]
    imp = jnp.concatenate([jnp.sum(p[g * NSA_REP:(g + 1) * NSA_REP], axis=0, keepdims=True)
                           for g in range(NSA_GROUPS)], axis=0)
    imp2 = _dot_exact(imp, pair_ref[...])
    width = imp2.shape[1]
    lane = lax.broadcasted_iota(jnp.int32, (NSA_GROUPS, width), 1)
    lanef = lane.astype(F32)
    cur = past // SEL_BLOCK
    forced = (lane == 0) | (lane == cur) | (lane == cur - 1)
    score = jnp.where(lane * SEL_BLOCK > past, NEG, jnp.where(forced, SEL_FORCE, imp2))
    out_lane = lax.broadcasted_iota(jnp.int32, (NSA_GROUPS, LANES), 1)
    idx = jnp.zeros((NSA_GROUPS, LANES), F32)
    for k in range(n_take):
        best = jnp.max(score, axis=-1, keepdims=True)
        pick = jnp.min(jnp.where(score == best, lanef, float(width)), axis=-1, keepdims=True)
        idx = jnp.where(out_lane == k, pick, idx)
        score = jnp.where(lanef == pick, 2.0 * NEG, score)
    idx_ref[0] = idx.astype(jnp.int32)


def _nsa_select(q16, kvc, cb, past):
    bs, n_heads, _ = q16.shape
    n_cmp = kvc.shape[0] // bs
    n_blocks = past // SEL_BLOCK + 1
    width = -(-n_blocks // LANES) * LANES
    n_take = min(N_SEL, n_blocks)
    pair = jnp.asarray((np.arange(n_cmp) // (SEL_BLOCK // CMP_BLOCK))[:, None] == np.arange(width)[None, :], BF16)
    body = functools.partial(_nsa_select_body, past=past, n_take=n_take)
    return pl.pallas_call(
        body, grid=(bs,),
        in_specs=[pl.BlockSpec((1, n_heads, HALF_W), lambda b: (b, 0, 0)),
                  pl.BlockSpec((n_cmp, KV_W), lambda b: (b, 0)),
                  pl.BlockSpec((n_heads, n_cmp), lambda b: (0, 0)),
                  pl.BlockSpec((n_cmp, width), lambda b: (0, 0))],
        out_specs=[pl.BlockSpec((1, n_heads, HEAD_DIM), lambda b: (b, 0, 0)),
                   pl.BlockSpec((1, NSA_GROUPS, LANES), lambda b: (b, 0, 0))],
        out_shape=[jax.ShapeDtypeStruct((bs, n_heads, HEAD_DIM), F32),
                   jax.ShapeDtypeStruct((bs, NSA_GROUPS, LANES), jnp.int32)],
        compiler_params=_params("parallel"), name="nsa_select")(q16, kvc, cb, pair), n_take


BIAS_TABLE = 2 * LANES


def _lookup_bias(tab, dist):
    n = dist.shape[1]
    onehot = (lax.broadcasted_iota(jnp.int32, (BIAS_TABLE, n), 0) == jnp.minimum(dist, BIAS_TABLE - 1))
    return _dot_exact(tab, jnp.where(onehot, 1.0, 0.0).astype(BF16))


def _nsa_attend_body(pt_ref, ix_ref, q_ref, seln_ref, winn_ref, wcol_ref, win_ref, oc_ref, g_ref, tab_ref, wtab_ref,
                     pick_ref, *rest, past, n_take):
    pages = rest[:n_take]
    o_ref, wout_ref = rest[n_take:]
    b = pl.program_id(0)
    g = pl.program_id(1)
    n_past_blocks = past // SEL_BLOCK
    page = pages[0].shape[2]
    q4 = q_ref[0, 0]
    q4f = q4.astype(F32)
    tab = tab_ref[0]
    gates = g_ref[0, 0]

    zs, kvs = [], []
    has_new = False
    lane = lax.broadcasted_iota(jnp.int32, (1, page), 1)
    for i in range(n_take):
        v = ix_ref[(b * NSA_GROUPS + g) * n_take + i]
        kvt = pages[i][0].astype(BF16)
        kpos = (v * SEL_BLOCK) // page * page + lane
        dist = past - kpos
        ok = (kpos // SEL_BLOCK == v) & (dist >= 0) & (v < n_past_blocks)
        z = _dot(q4, kvt) + _lookup_bias(tab, jnp.maximum(dist, 0))
        zs.append(jnp.where(ok, z, NEG))
        kvs.append(kvt)
        has_new = jnp.logical_or(has_new, v >= n_past_blocks)
    new_s = _bf16_round(seln_ref[0])
    z_new = jnp.sum(q4f * new_s, axis=-1, keepdims=True) + tab[:, :1]
    z_new = jnp.where(has_new, z_new, NEG)
    m = z_new
    for z in zs:
        m = jnp.maximum(m, jnp.max(z, axis=-1, keepdims=True))
    es = [jnp.exp(z - m) for z in zs]
    e_new = jnp.exp(z_new - m)
    den = e_new
    for e in es:
        den = den + jnp.sum(e, axis=-1, keepdims=True)
    den = jnp.maximum(den, 1e-30)
    o_s = _bf16_round(e_new / den) * new_s
    for e, kvt in zip(es, kvs):
        o_s = o_s + _dot_nt((e / den).astype(BF16), kvt)

    w = win_ref[0]
    wb = w.astype(BF16)
    new_w = _bf16_round(winn_ref[0])
    n_win = w.shape[1]
    zw = _dot(q4, wb) + wtab_ref[0][:, :n_win]
    zw_new = jnp.sum(q4f * new_w, axis=-1, keepdims=True) + wtab_ref[0][:, n_win:n_win + 1]
    mw = jnp.maximum(jnp.max(zw, axis=-1, keepdims=True), zw_new)
    ew = jnp.exp(zw - mw)
    ew_new = jnp.exp(zw_new - mw)
    denw = jnp.maximum(jnp.sum(ew, axis=-1, keepdims=True) + ew_new, 1e-30)
    o_w = _dot_nt((ew / denw).astype(BF16), wb) + _bf16_round(ew_new / denw) * new_w

    o_sw = gates[:, 1:2] * o_s + gates[:, 2:3] * o_w
    o_ref[0, 0] = gates[:, 0:1] * oc_ref[0, 0] + _dot_exact(o_sw, pick_ref[0])

    @pl.when(g == 0)
    def _():
        cols = lax.broadcasted_iota(jnp.int32, w.shape, 1)
        wout_ref[0] = jnp.where(cols == n_win - 1, wcol_ref[0], pltpu.roll(w, n_win - 1, 1))


def _nsa_attend(q4, sel_new, win_new, state_win, layer, cache_sel, page_table, idx, n_take, oc, gates, rel_bias):
    bs = q4.shape[0]
    n_pool, page = cache_sel.shape[1:3]
    n_pages = page_table.shape[1]
    past = n_pages * page
    n_past_blocks = past // SEL_BLOCK
    n_win = state_win.shape[2]
    assert _bucket_saturates_from(BIAS_TABLE - 1, past + SEL_BLOCK) and page % SEL_BLOCK == 0
    pages = _position_minor_pages(cache_sel)
    base = layer * n_pool
    win_t = jnp.transpose(state_win, (0, 1, 3, 4, 5, 2)).reshape(state_win.shape[0] * bs, KV_W, n_win)
    tab = _bias_of(rel_bias, np.arange(BIAS_TABLE)).reshape(NSA_GROUPS, NSA_REP, BIAS_TABLE)
    wpad = -(-(n_win + 1) // LANES) * LANES
    wdist = np.maximum(n_win - np.arange(wpad), 0)
    wtab = _bias_of(rel_bias, wdist).reshape(NSA_GROUPS, NSA_REP, wpad)
    feats = np.arange(KV_W)[None, :, None]
    pick = jnp.asarray(feats == HALF_W + np.arange(NSA_GROUPS)[:, None, None] * HEAD_DIM
                       + np.arange(HEAD_DIM)[None, None, :], BF16)

    def page_spec(i):
        def index(b, g, pt, ix):
            v = jnp.minimum(ix[(b * NSA_GROUPS + g) * n_take + i], n_past_blocks - 1)
            return (base + pt[b, (v * SEL_BLOCK) // page], 0, 0)
        return pl.BlockSpec((1, KV_W, page), index)

    grid_spec = pltpu.PrefetchScalarGridSpec(
        num_scalar_prefetch=2, grid=(bs, NSA_GROUPS),
        in_specs=[pl.BlockSpec((1, 1, NSA_REP, KV_W), lambda b, g, pt, ix: (b, g, 0, 0)),
                  pl.BlockSpec((1, 1, KV_W), lambda b, g, pt, ix: (b, 0, 0)),
                  pl.BlockSpec((1, 1, KV_W), lambda b, g, pt, ix: (b, 0, 0)),
                  pl.BlockSpec((1, KV_W, 1), lambda b, g, pt, ix: (b, 0, 0)),
                  pl.BlockSpec((1, KV_W, n_win), lambda b, g, pt, ix: (layer * bs + b, 0, 0)),
                  pl.BlockSpec((1, 1, NSA_REP, HEAD_DIM), lambda b, g, pt, ix: (b, g, 0, 0)),
                  pl.BlockSpec((1, 1, NSA_REP, LANES), lambda b, g, pt, ix: (b, g, 0, 0)),
                  pl.BlockSpec((1, NSA_REP, BIAS_TABLE), lambda b, g, pt, ix: (g, 0, 0)),
                  pl.BlockSpec((1, NSA_REP, wpad), lambda b, g, pt, ix: (g, 0, 0)),
                  pl.BlockSpec((1, KV_W, HEAD_DIM), lambda b, g, pt, ix: (g, 0, 0))]
                 + [page_spec(i) for i in range(n_take)],
        out_specs=[pl.BlockSpec((1, 1, NSA_REP, HEAD_DIM), lambda b, g, pt, ix: (b, g, 0, 0)),
                   pl.BlockSpec((1, KV_W, n_win), lambda b, g, pt, ix: (b, 0, 0))])
    body = functools.partial(_nsa_attend_body, past=past, n_take=n_take)
    return pl.pallas_call(
        body, grid_spec=grid_spec,
        out_shape=[jax.ShapeDtypeStruct((bs, NSA_GROUPS, NSA_REP, HEAD_DIM), F32),
                   jax.ShapeDtypeStruct((bs, KV_W, n_win), F32)],
        compiler_params=_params("parallel", "arbitrary"), name="nsa_attend")(
            page_table, idx, q4, sel_new, win_new, win_new.reshape(bs, KV_W, 1), win_t, oc, gates, tab, wtab, pick,
            *([pages] * n_take))


def _compress_weights(cmp_pos, w_cmp):
    pos = jnp.broadcast_to(jnp.swapaxes(cmp_pos, 0, 1)[:, :, None, :], (CMP_BLOCK, 2, NSA_GROUPS, HEAD_DIM))
    eye = jnp.eye(NSA_GROUPS, dtype=w_cmp.dtype)
    w2 = jnp.einsum('clde,gh->lcgdhe', w_cmp, eye).reshape(CMP_BLOCK, 2, HALF_W, HALF_W)
    return pos.reshape(CMP_BLOCK, KV_W), w2.astype(BF16)


def _nsa_gate_weights(wg, paired):
    d = wg.shape[0]
    if paired:
        w = wg.reshape(d, 3, 2, 2 * NSA_REP).transpose(0, 2, 1, 3).reshape(d, 2, 3 * 2 * NSA_REP)
        w = jnp.pad(w, ((0, 0), (0, 0), (0, LANES - w.shape[2])))
        return w.reshape(d, 2 * LANES).astype(BF16)
    return jnp.pad(wg, ((0, 0), (0, LANES - wg.shape[1]))).astype(BF16)


def _rows_by_position(x_t, lead):
    batch, _, t = x_t.shape
    nd = len(lead)
    return jnp.transpose(x_t.reshape(batch, *lead, t), (0, nd + 1) + tuple(range(1, nd + 1)))


def kernel(x_prompt, x_sample, cache_sb_kv, cache_cmp_kv, cache_sel_kv, state_win_kv, page_table,
           w_in_sb, w_out_sb, w_in_nsa, w_out_nsa, cmp_pos, w_cmp, rel_bias, ln_g, ln_b, w_up, w_down):
    bp, t, d = x_prompt.shape
    bs, tn, _ = x_sample.shape
    assert tn == 1 and d == NSA_GROUPS * NSA_REP * HEAD_DIM
    depth = ln_g.shape[0]
    alpha = (2 * depth) ** 0.25
    past = page_table.shape[1] * cache_sb_kv.shape[2]
    n_heads = d // HEAD_DIM
    sb_lead = (2, n_heads, HEAD_DIM)
    nsa_lead = (2, NSA_GROUPS, HEAD_DIM)
    xp = x_prompt.reshape(bp * t, d)
    xs = x_sample.reshape(bs, d)
    sb_p, sb_s, cmp_p, cmp_s, sel_p, sel_s, win_p, win_s = [], [], [], [], [], [], [], []
    for i in range(depth):
        l = i // 2
        if i % 2 == 0:
            w = w_in_sb[l].astype(BF16)
            wq, wkv = w[:, :d], w[:, d:]
            q, v, kvt, kt = _sb_proj(xp, wq, w[:, 2 * d:], wkv.T, bp)
            mp = _sb_prompt(q, kt, v, bp)
            qs, kvs = _sb_proj_rows(xs, wq, wkv)
            ms = _sb_sample(qs, cache_sb_kv, l, page_table)
            sb_p.append(_rows_by_position(kvt, sb_lead))
            sb_s.append(kvs.reshape(bs, tn, *sb_lead))
            w_out = w_out_sb[l]
        else:
            w = w_in_nsa[l].astype(BF16)
            wq = w[:, :d]
            wkv = w[:, d:d + 3 * KV_W]
            wg = w_in_nsa[l][:, d + 3 * KV_W:]
            pos_rows, w2 = _compress_weights(cmp_pos[l], w_cmp[l])
            q, c_rows, gates, c_t, s_t, w_t, s_tb, w_tb = _nsa_proj(
                xp, wq, wkv[:, :KV_W], _nsa_gate_weights(wg, True), wkv.T, bp)
            kvc = _compress(c_rows.reshape(bp * t // CMP_BLOCK, CMP_BLOCK * KV_W), pos_rows.reshape(1, -1), w2)
            mp = _nsa_prompt(q, kvc, s_tb, w_tb, gates, rel_bias, bp)
            w_buf = min(WINDOW, t)
            cmp_p.append(_rows_by_position(c_t, nsa_lead))
            sel_p.append(_rows_by_position(s_t, nsa_lead))
            win_p.append(_rows_by_position(w_t[:, :, t - w_buf:], nsa_lead))
            qs, kv_s, gates_s = _nsa_proj_rows(xs, wq, wkv, _nsa_gate_weights(wg, False))
            c_s, s_s, w_s = kv_s[:, :KV_W], kv_s[:, KV_W:2 * KV_W], kv_s[:, 2 * KV_W:]
            kvc_s = _compress_paged(cache_cmp_kv, l, page_table, pos_rows, w2)
            qh = qs.reshape(bs, NSA_GROUPS, NSA_REP, 1, HEAD_DIM)
            eye = jnp.eye(NSA_GROUPS, dtype=qs.dtype)[None, :, None, :, None]
            q16 = (qh * eye).reshape(bs, n_heads, HALF_W)
            q4 = jnp.pad(q16, ((0, 0), (0, 0), (0, HALF_W))).reshape(bs, NSA_GROUPS, NSA_REP, KV_W)
            n_cmp = past // CMP_BLOCK
            cb_s = _bias_of(rel_bias, past - (np.arange(n_cmp) * CMP_BLOCK + CMP_BLOCK - 1))
            (oc, idx), n_take = _nsa_select(q16, kvc_s, cb_s, past)
            g3 = gates_s[:, :3 * n_heads].reshape(bs, 3, NSA_GROUPS, NSA_REP).transpose(0, 2, 3, 1)
            g3 = jnp.pad(g3, ((0, 0), (0, 0), (0, 0), (0, LANES - 3)))
            o4, wnew = _nsa_attend(q4, s_s.reshape(bs, 1, KV_W), w_s.reshape(bs, 1, KV_W), state_win_kv, l,
                                   cache_sel_kv, page_table, idx[:, :, :n_take].reshape(-1), n_take,
                                   oc.reshape(bs, NSA_GROUPS, NSA_REP, HEAD_DIM), g3, rel_bias)
            ms = o4.reshape(bs, d)
            cmp_s.append(c_s.reshape(bs, tn, *nsa_lead))
            sel_s.append(s_s.reshape(bs, tn, *nsa_lead))
            win_s.append(_rows_by_position(wnew, nsa_lead))
            w_out = w_out_nsa[l]
        tail_args = (w_out.astype(BF16), ln_g[i, 0][None], ln_b[i, 0][None], w_up[i].astype(BF16),
                     w_down[i].astype(BF16), ln_g[i, 1][None], ln_b[i, 1][None], alpha)
        xp = _tail(xp, mp, *tail_args)
        xs = _tail(xs, ms, *tail_args)
    return (xp.reshape(bp, t, d), xs.reshape(bs, tn, d), jnp.stack(sb_p), jnp.stack(sb_s), jnp.stack(cmp_p),
            jnp.stack(cmp_s), jnp.stack(sel_p), jnp.stack(sel_s), jnp.stack(win_p), jnp.stack(win_s))
```

```python
import functools
import math

import numpy as np
import jax
import jax.numpy as jnp
from jax import lax
from jax.experimental import pallas as pl
from jax.experimental.pallas import tpu as pltpu

F32 = jnp.float32
BF16 = jnp.bfloat16

HEAD_DIM = 64
NSA_GROUPS = 4
NSA_REP = 4
CMP_BLOCK = 32
SEL_BLOCK = 64
N_SEL = 16
WINDOW = 512
N_BUCKETS = 32
MAX_DISTANCE = 128
LN_EPS = 1e-5
SEL_FORCE = 1e4
NEG = -1e30

LANES = 128
ROW_TILE = 256
SB_TILE = 256
NSA_TQ = 256
SB_PAGES_PER_STEP = 16
CMP_PAGES_PER_STEP = 8
VMEM_LIMIT = 56 * 1024 * 1024

KV_W = 2 * NSA_GROUPS * HEAD_DIM
HALF_W = NSA_GROUPS * HEAD_DIM

_NT = (((1,), (1,)), ((), ()))


def _dot(a, b):
    return jnp.dot(a, b, preferred_element_type=F32)


def _dot_nt(a, b):
    return lax.dot_general(a, b, _NT, preferred_element_type=F32)


def _split2(x):
    hi = x.astype(BF16)
    lo = (x - hi.astype(F32)).astype(BF16)
    return hi, lo


def _split3(x):
    hi = x.astype(BF16)
    r = x - hi.astype(F32)
    mid = r.astype(BF16)
    lo = (r - mid.astype(F32)).astype(BF16)
    return hi, mid, lo


def _dot_exact(x, m):
    hi, mid, lo = _split3(x)
    return _dot(hi, m) + _dot(mid, m) + _dot(lo, m)


def _softplus(z):
    return jnp.maximum(z, 0.0) + jnp.log(1.0 + jnp.exp(-jnp.abs(z)))


def _layer_norm(y, g, b):
    mu = jnp.mean(y, axis=-1, keepdims=True)
    yc = y - mu
    var = jnp.mean(yc * yc, axis=-1, keepdims=True)
    return yc * lax.rsqrt(var + LN_EPS) * g + b


def _bf16_round(x):
    return x.astype(BF16).astype(F32)


def _params(*sem):
    return pltpu.CompilerParams(dimension_semantics=sem, vmem_limit_bytes=VMEM_LIMIT)


def _const_spec(a):
    nd = a.ndim
    return pl.BlockSpec(a.shape, lambda *_: (0,) * nd)


def _row_call(body, row_args, const_args, outs, name, t_outs=(), seq_len=None):
    m = row_args[0].shape[0]
    tm = min(ROW_TILE, m)
    assert m % tm == 0
    in_specs = [pl.BlockSpec((tm, a.shape[1]), lambda i: (i, 0)) for a in row_args]
    in_specs += [_const_spec(a) for a in const_args]
    out_specs = [pl.BlockSpec((tm, o.shape[1]), lambda i: (i, 0)) for o in outs]
    if t_outs:
        nt = seq_len // tm
        assert seq_len % tm == 0
        out_specs += [pl.BlockSpec((1, o.shape[1], tm), lambda i: (i // nt, 0, i % nt)) for o in t_outs]
    return pl.pallas_call(
        body, grid=(m // tm,), in_specs=in_specs, out_specs=out_specs, out_shape=list(outs) + list(t_outs),
        compiler_params=_params("parallel"), name=name)(*row_args, *const_args)


def _sb_proj_rows_body(x_ref, wq_ref, wkv_ref, q_ref, kv_ref):
    xb = x_ref[...].astype(BF16)
    q_ref[...] = (_dot(xb, wq_ref[...]) * HEAD_DIM ** -0.5).astype(BF16)
    kv_ref[...] = _dot(xb, wkv_ref[...])


def _sb_proj_rows(x, wq, wkv):
    m, d = x.shape
    outs = [jax.ShapeDtypeStruct((m, d), BF16), jax.ShapeDtypeStruct((m, 2 * d), F32)]
    return _row_call(_sb_proj_rows_body, [x], [wq, wkv], outs, "sb_proj_rows")


def _sb_proj_body(x_ref, wq_ref, wv_ref, wkvt_ref, q_ref, v_ref, kvt_ref, kt_ref):
    xb = x_ref[...].astype(BF16)
    q_ref[...] = (_dot(xb, wq_ref[...]) * HEAD_DIM ** -0.5).astype(BF16)
    v_ref[...] = _dot(xb, wv_ref[...]).astype(BF16)
    kvt = _dot_nt(wkvt_ref[...], xb)
    kvt_ref[0] = kvt
    kt_ref[0] = kvt[:kt_ref.shape[1]].astype(BF16)


def _sb_proj(x, wq, wv, wkvt, batch):
    m, d = x.shape
    t = m // batch
    outs = [jax.ShapeDtypeStruct((m, d), BF16), jax.ShapeDtypeStruct((m, d), BF16)]
    t_outs = [jax.ShapeDtypeStruct((batch, 2 * d, t), F32), jax.ShapeDtypeStruct((batch, d, t), BF16)]
    return _row_call(_sb_proj_body, [x], [wq, wv, wkvt], outs, "sb_proj", t_outs, t)


def _tail_body(x_ref, m_ref, wo_ref, g1_ref, b1_ref, wu_ref, wd_ref, g2_ref, b2_ref, o_ref, *, alpha):
    y = alpha * x_ref[...] + _dot(m_ref[...].astype(BF16), wo_ref[...])
    y = _layer_norm(y, g1_ref[...], b1_ref[...])
    h = jnp.maximum(_dot(y.astype(BF16), wu_ref[...]), 0.0)
    y = alpha * y + _dot((h * h).astype(BF16), wd_ref[...])
    o_ref[...] = _layer_norm(y, g2_ref[...], b2_ref[...])


def _tail(x, mix, wo, g1, b1, wu, wd, g2, b2, alpha):
    outs = [jax.ShapeDtypeStruct(x.shape, F32)]
    body = functools.partial(_tail_body, alpha=alpha)
    return _row_call(body, [x, mix], [wo, g1, b1, wu, wd, g2, b2], outs, "block_tail")[0]


def _rev_cumsum_matrix(n):
    j = np.arange(n)
    u = j[:, None] >= j[None, :]
    return jnp.asarray(np.concatenate([u, u], axis=0), BF16)


def _sb_local(z, u, mask):
    sp = _softplus(z)
    if mask is not None:
        sp = jnp.where(mask, sp, 0.0)
    cum = _dot(jnp.concatenate(_split2(sp), axis=1), u)
    return z - cum, cum[:, :1]


def _sb_weights(d, suf, mask):
    a = jnp.exp(d - suf)
    if mask is not None:
        a = jnp.where(mask, a, 0.0)
    return a.astype(BF16)


def _sb_prompt_body(q_ref, kt_ref, v_ref, u_ref, o_ref, acc_ref):
    tq = q_ref.shape[0]
    qi = pl.program_id(2)
    lane = lax.broadcasted_iota(jnp.int32, (tq, LANES), 1)
    lo_half = lane < HEAD_DIM
    q2 = q_ref[...]
    zero = jnp.zeros_like(q2)
    qs = (jnp.where(lo_half, q2, zero), jnp.where(lo_half, zero, q2))
    u = u_ref[...]
    row = lax.broadcasted_iota(jnp.int32, (tq, tq), 0)
    col = lax.broadcasted_iota(jnp.int32, (tq, tq), 1)
    strictly_before = col < row
    acc_ref[...] = jnp.zeros_like(acc_ref)

    def tiles(js, sufs, mask):
        local = []
        for j in js:
            start = pl.multiple_of(j * tq, tq)
            kt2 = kt_ref[0, :, pl.ds(start, tq)]
            v2 = v_ref[pl.ds(start, tq), :]
            local.append((v2, [_sb_local(_dot(qs[h], kt2), u, mask) for h in range(2)]))
        sufs = list(sufs)
        for v2, per_head in local:
            for h, (d, tot) in enumerate(per_head):
                acc_ref[h] += _dot(_sb_weights(d, sufs[h], mask), v2)
                sufs[h] = sufs[h] + tot
        return tuple(sufs)

    zero_suf = jnp.zeros((tq, 1), F32)
    sufs = tiles([qi], (zero_suf, zero_suf), strictly_before)
    sufs = lax.fori_loop(0, qi // 2, lambda jj, s: tiles([qi - 1 - 2 * jj, qi - 2 - 2 * jj], s, None), sufs)

    @pl.when(qi % 2 == 1)
    def _():
        tiles([0], sufs, None)

    o_ref[...] = jnp.where(lo_half, acc_ref[0], acc_ref[1]).astype(BF16)


def _sb_prompt(q, kt, v, batch):
    m, d = q.shape
    t = m // batch
    tq = min(SB_TILE, t)
    assert t % tq == 0 and d % LANES == 0
    nq = t // tq
    u = _rev_cumsum_matrix(tq)
    return pl.pallas_call(
        _sb_prompt_body, grid=(batch, d // LANES, nq),
        in_specs=[pl.BlockSpec((tq, LANES), lambda b, h, i: (b * nq + i, h)),
                  pl.BlockSpec((1, LANES, t), lambda b, h, i: (b, h, 0)),
                  pl.BlockSpec((t, LANES), lambda b, h, i: (b, h)),
                  pl.BlockSpec((2 * tq, tq), lambda b, h, i: (0, 0))],
        out_specs=pl.BlockSpec((tq, LANES), lambda b, h, i: (b * nq + i, h)),
        out_shape=jax.ShapeDtypeStruct((m, d), BF16),
        scratch_shapes=[pltpu.VMEM((2, tq, LANES), F32)],
        compiler_params=_params("parallel", "parallel", "arbitrary"), name="sb_prompt")(q, kt, v, u)


def _position_minor_pages(cache):
    n_layers, n_pool, page = cache.shape[:3]
    feat = cache.shape[3] * cache.shape[4] * cache.shape[5]
    return jnp.transpose(cache, (0, 1, 3, 4, 5, 2)).reshape(n_layers * n_pool, feat, page)


def _sb_sample_body(pt_ref, q_ref, u_ref, *rest, pages_per_step):
    pages = rest[:pages_per_step]
    o_ref, acc_ref, suf_ref = rest[pages_per_step:]
    s = pl.program_id(1)
    d = q_ref.shape[-1]
    heads = d // HEAD_DIM

    @pl.when(s == 0)
    def _():
        acc_ref[...] = jnp.zeros_like(acc_ref)
        suf_ref[...] = jnp.zeros_like(suf_ref)

    own = (lax.broadcasted_iota(jnp.int32, (heads, d), 1) // HEAD_DIM
           == lax.broadcasted_iota(jnp.int32, (heads, d), 0))
    qrow = jnp.broadcast_to(q_ref[0].astype(F32), (heads, d))
    qbd = jnp.where(own, qrow, 0.0).astype(BF16)
    u = u_ref[...]
    acc = acc_ref[...]
    suf = suf_ref[...]
    local = []
    for p in range(pages_per_step):
        kt = pages[p][0, :d, :].astype(BF16)
        local.append(_sb_local(_dot(qbd, kt), u, None))
    for p, (dl, tot) in enumerate(local):
        vt = pages[p][0, d:, :].astype(BF16)
        acc = acc + _dot_nt(_sb_weights(dl, suf, None), vt)
        suf = suf + tot
    acc_ref[...] = acc
    suf_ref[...] = suf

    @pl.when(s == pl.num_programs(1) - 1)
    def _():
        o_ref[0] = jnp.sum(jnp.where(own, acc, 0.0), axis=0, keepdims=True)


def _sb_sample(q, cache, layer, page_table):
    b, d = q.shape
    n_pool, page = cache.shape[1:3]
    assert cache.shape[3] * cache.shape[4] * cache.shape[5] == 2 * d
    n_pages = page_table.shape[1]
    pps = math.gcd(SB_PAGES_PER_STEP, n_pages)
    pages = _position_minor_pages(cache)
    base = layer * n_pool

    def page_spec(p):
        return pl.BlockSpec((1, 2 * d, page),
                            lambda i, s, pt: (base + pt[i, n_pages - 1 - (s * pps + p)], 0, 0))

    grid_spec = pltpu.PrefetchScalarGridSpec(
        num_scalar_prefetch=1, grid=(b, n_pages // pps),
        in_specs=[pl.BlockSpec((1, 1, d), lambda i, s, pt: (i, 0, 0)),
                  pl.BlockSpec((2 * page, page), lambda i, s, pt: (0, 0))]
                 + [page_spec(p) for p in range(pps)],
        out_specs=pl.BlockSpec((1, 1, d), lambda i, s, pt: (i, 0, 0)),
        scratch_shapes=[pltpu.VMEM((d // HEAD_DIM, d), F32), pltpu.VMEM((d // HEAD_DIM, 1), F32)])
    out = pl.pallas_call(
        functools.partial(_sb_sample_body, pages_per_step=pps), grid_spec=grid_spec,
        out_shape=jax.ShapeDtypeStruct((b, 1, d), F32),
        compiler_params=_params("parallel", "arbitrary"), name="sb_sample")(
            page_table, q.reshape(b, 1, d), _rev_cumsum_matrix(page), *([pages] * pps))
    return out.reshape(b, d)


def _nsa_proj_rows_body(x_ref, wq_ref, wkv_ref, wg_ref, q_ref, kv_ref, g_ref):
    xb = x_ref[...].astype(BF16)
    q_ref[...] = (_dot(xb, wq_ref[...]) * HEAD_DIM ** -0.5).astype(BF16)
    kv_ref[...] = _dot(xb, wkv_ref[...])
    g_ref[...] = jax.nn.sigmoid(_dot(xb, wg_ref[...]))


def _nsa_proj_rows(x, wq, wkv, wg):
    m, d = x.shape
    outs = [jax.ShapeDtypeStruct((m, d), BF16), jax.ShapeDtypeStruct((m, wkv.shape[1]), F32),
            jax.ShapeDtypeStruct((m, wg.shape[1]), F32)]
    return _row_call(_nsa_proj_rows_body, [x], [wq, wkv, wg], outs, "nsa_proj_rows")


def _nsa_proj_body(x_ref, wq_ref, wc_ref, wg_ref, wkvt_ref, q_ref, cmp_ref, g_ref,
                   cmpt_ref, selt_ref, wint_ref, seltb_ref, wintb_ref):
    xb = x_ref[...].astype(BF16)
    q_ref[...] = (_dot(xb, wq_ref[...]) * HEAD_DIM ** -0.5).astype(BF16)
    cmp_ref[...] = _dot(xb, wc_ref[...])
    g_ref[...] = jax.nn.sigmoid(_dot(xb, wg_ref[...]))
    kvt = _dot_nt(wkvt_ref[...], xb)
    cmpt_ref[0] = kvt[:KV_W]
    sel = kvt[KV_W:2 * KV_W]
    win = kvt[2 * KV_W:]
    selt_ref[0] = sel
    wint_ref[0] = win
    seltb_ref[0] = sel.astype(BF16)
    wintb_ref[0] = win.astype(BF16)


def _nsa_proj(x, wq, wc, wg, wkvt, batch):
    m, d = x.shape
    t = m // batch
    outs = [jax.ShapeDtypeStruct((m, d), BF16), jax.ShapeDtypeStruct((m, KV_W), F32),
            jax.ShapeDtypeStruct((m, wg.shape[1]), F32)]
    t_outs = [jax.ShapeDtypeStruct((batch, KV_W, t), F32)] * 3 + [jax.ShapeDtypeStruct((batch, KV_W, t), BF16)] * 2
    return _row_call(_nsa_proj_body, [x], [wq, wc, wg, wkvt], outs, "nsa_proj", t_outs, t)


CMP_L_CHUNK = 4


def _compress_body(x_ref, pos_ref, w_ref, o_ref, acc_ref):
    kc = pl.program_id(1)

    @pl.when(kc == 0)
    def _():
        acc_ref[...] = jnp.zeros_like(acc_ref)

    xb = (x_ref[...] + pos_ref[...]).astype(BF16)
    for l in range(CMP_L_CHUNK):
        for c in range(2):
            lo = l * KV_W + c * HALF_W
            acc_ref[:, c * HALF_W:(c + 1) * HALF_W] += _dot(xb[:, lo:lo + HALF_W], w_ref[l, c])

    @pl.when(kc == pl.num_programs(1) - 1)
    def _():
        o_ref[...] = acc_ref[...].astype(BF16)


def _compress(x2, pos_flat, w2):
    rows, width = x2.shape
    tm = min(512, rows)
    assert rows % tm == 0 and CMP_BLOCK % CMP_L_CHUNK == 0
    kw = CMP_L_CHUNK * KV_W
    return pl.pallas_call(
        _compress_body, grid=(rows // tm, width // kw),
        in_specs=[pl.BlockSpec((tm, kw), lambda i, k: (i, k)),
                  pl.BlockSpec((1, kw), lambda i, k: (0, k)),
                  pl.BlockSpec((CMP_L_CHUNK, 2, HALF_W, HALF_W), lambda i, k: (k, 0, 0, 0))],
        out_specs=pl.BlockSpec((tm, KV_W), lambda i, k: (i, 0)),
        out_shape=jax.ShapeDtypeStruct((rows, KV_W), BF16),
        scratch_shapes=[pltpu.VMEM((tm, KV_W), F32)],
        compiler_params=_params("parallel", "arbitrary"), name="nsa_compress")(x2, pos_flat, w2)


def _compress_paged_body(pt_ref, pos_ref, perm_ref, w_ref, *rest, pages_per_step, steps_per_dot):
    pages = rest[:pages_per_step]
    o_ref, x_ref = rest[pages_per_step:]
    s = pl.program_id(1)
    slot = s % steps_per_dot
    pos = pos_ref[...]
    group = perm_ref.shape[0] // pages[0].shape[2]
    slab = perm_ref.shape[0] // CMP_BLOCK
    r0 = pl.multiple_of(slot * (pages_per_step // group) * slab, slab)
    for pg in range(pages_per_step // group):
        a = jnp.concatenate([pages[pg * group + k][0] + pos for k in range(group)], axis=1).astype(BF16)
        xp = _dot_nt(perm_ref[...], a)
        for l in range(CMP_BLOCK):
            x_ref[l, pl.ds(r0 + pg * slab, slab), :] = xp[l * slab:(l + 1) * slab, :]

    @pl.when(slot == steps_per_dot - 1)
    def _():
        for c in range(2):
            xc = jnp.concatenate([x_ref[l, :, c * HALF_W:(c + 1) * HALF_W].astype(BF16)
                                  for l in range(CMP_BLOCK)], axis=1)
            o_ref[:, c * HALF_W:(c + 1) * HALF_W] = _dot(xc, w_ref[c]).astype(BF16)


def _compress_paged(cache, layer, page_table, pos_rows, w2):
    n_pool, page = cache.shape[1:3]
    bs, n_pages = page_table.shape
    rpp = page // CMP_BLOCK
    pages = _position_minor_pages(cache)
    base = layer * n_pool
    pps = math.gcd(CMP_PAGES_PER_STEP, n_pages)
    spd = math.gcd(8, n_pages // pps)
    rows = pps * spd * rpp
    n_steps = n_pages // pps
    group = 8 // rpp
    assert 8 % rpp == 0 and pps % group == 0
    pos_t = jnp.tile(pos_rows, (rpp, 1)).T
    l_, k_, n_ = np.meshgrid(np.arange(CMP_BLOCK), np.arange(group), np.arange(rpp), indexing="ij")
    perm = np.zeros((group * page, group * page), np.float32)
    perm[(l_ * group * rpp + k_ * rpp + n_).ravel(), (k_ * page + n_ * CMP_BLOCK + l_).ravel()] = 1.0

    def page_spec(p):
        return pl.BlockSpec((1, KV_W, page), lambda i, s, pt: (base + pt[i, s * pps + p], 0, 0))

    grid_spec = pltpu.PrefetchScalarGridSpec(
        num_scalar_prefetch=1, grid=(bs, n_steps),
        in_specs=[pl.BlockSpec((KV_W, page), lambda i, s, pt: (0, 0)),
                  pl.BlockSpec(perm.shape, lambda i, s, pt: (0, 0)),
                  pl.BlockSpec((2, CMP_BLOCK * HALF_W, HALF_W), lambda i, s, pt: (0, 0, 0))]
                 + [page_spec(p) for p in range(pps)],
        out_specs=pl.BlockSpec((rows, KV_W), lambda i, s, pt: (i * (n_steps // spd) + s // spd, 0)),
        scratch_shapes=[pltpu.VMEM((CMP_BLOCK, rows, KV_W), F32)])
    body = functools.partial(_compress_paged_body, pages_per_step=pps, steps_per_dot=spd)
    return pl.pallas_call(
        body, grid_spec=grid_spec, out_shape=jax.ShapeDtypeStruct((bs * n_pages * rpp, KV_W), BF16),
        compiler_params=_params("parallel", "arbitrary"), name="nsa_compress_paged")(
            page_table, pos_t, jnp.asarray(perm, BF16),
            jnp.swapaxes(w2, 0, 1).reshape(2, CMP_BLOCK * HALF_W, HALF_W), *([pages] * pps))


def _rel_bucket(dist):
    dist = jnp.maximum(dist, 0)
    max_exact = N_BUCKETS // 2
    dd = jnp.maximum(dist, 1).astype(F32)
    large = max_exact + (jnp.log(dd / max_exact) / math.log(MAX_DISTANCE / max_exact)
                         * (N_BUCKETS - max_exact)).astype(jnp.int32)
    large = jnp.minimum(large, N_BUCKETS - 1)
    return jnp.where(dist < max_exact, dist, large)


def _bias_of(rel_bias, dist):
    onehot = jax.nn.one_hot(_rel_bucket(jnp.asarray(dist, jnp.int32)), N_BUCKETS, dtype=F32)
    return jnp.einsum('...k,kh->h...', onehot, rel_bias.astype(F32), precision=lax.Precision.HIGHEST)


def _bucket_saturates_from(d0, d1):
    dd = np.arange(d0, d1 + 1).astype(np.float32)
    max_exact = N_BUCKETS // 2
    large = max_exact + (np.log(dd / np.float32(max_exact)) / np.float32(math.log(MAX_DISTANCE / max_exact))
                         * np.float32(N_BUCKETS - max_exact)).astype(np.int32)
    return bool(np.all(large >= N_BUCKETS - 1)) and d0 >= max_exact


def _nsa_prompt_body(q_ref, kc_ref, vc_ref, ks_ref, vs_ref, kw_ref, vw_ref, g_ref, near_ref, far_ref, edge_ref,
                     cb_ref, e_ref, pair_ref, o_ref, qh_ref, selx_ref, zs_ref, mx_ref, den_ref, acc_ref, out_ref,
                     *, n_near):
    tq = q_ref.shape[0]
    nc = kc_ref.shape[0]
    rep = NSA_REP
    n_heads = 2 * rep
    qi = pl.program_id(2)
    lane = lax.broadcasted_iota(jnp.int32, (tq, LANES), 1)
    rowl = lax.broadcasted_iota(jnp.int32, (tq, LANES), 0)
    lo_half = lane < HEAD_DIM
    half_mask = (lo_half, jnp.logical_not(lo_half))
    gates = g_ref[...]

    def gate(branch, i):
        k = branch * n_heads + i
        return gates[:, k:k + 1]

    def rows(r):
        return slice(r * tq, (r + 1) * tq)

    for i in range(n_heads):
        hf, r = divmod(i, rep)
        qf = q_ref[:, (i // 2) * LANES:(i // 2 + 1) * LANES].astype(F32)
        if i % 2 != hf:
            qf = pltpu.roll(qf, HEAD_DIM, 1)
        qh_ref[hf, rows(r), :] = jnp.where(half_mask[hf], qf, 0.0).astype(BF16)

    coln = lax.broadcasted_iota(jnp.int32, (tq, nc), 1)
    rown = lax.broadcasted_iota(jnp.int32, (tq, nc), 0)
    valid_c = qi * tq + rown - (coln * CMP_BLOCK + CMP_BLOCK - 1) >= 0
    m_rel = jnp.clip((qi + 1) * (tq // CMP_BLOCK) - 1 - coln[:1], 0, n_near)
    pick_col = jnp.where(lax.broadcasted_iota(jnp.int32, (LANES, nc), 0) == m_rel, 1.0, 0.0).astype(BF16)
    kc = kc_ref[...]
    vc = vc_ref[...]
    n_blocks = ks_ref.shape[2] // SEL_BLOCK
    nb8 = -(-n_blocks // 8) * 8
    blk_t = lax.broadcasted_iota(jnp.int32, (nb8, tq), 0)
    for hf in range(2):
        imp = jnp.zeros((tq, nc), F32)
        for r in range(rep):
            i = hf * rep + r
            cbias = _dot_exact(cb_ref[i], pick_col)
            z = jnp.where(valid_c, _dot_nt(qh_ref[hf, rows(r), :], kc) + cbias, NEG)
            e = jnp.where(valid_c, jnp.exp(z - jnp.max(z, axis=-1, keepdims=True)), 0.0)
            p = e / jnp.maximum(jnp.sum(e, axis=-1, keepdims=True), 1e-30)
            imp = imp + p
            out_ref[i] = gate(0, i) * _dot(p.astype(BF16), vc)
        imp2 = _dot_exact(imp, pair_ref[...])
        qpos = qi * tq + rowl
        cur = qpos // SEL_BLOCK
        forced = (lane == 0) | (lane == cur) | (lane == cur - 1)
        score = jnp.where(lane * SEL_BLOCK > qpos, NEG, jnp.where(forced, SEL_FORCE, imp2))
        st = score.T[:nb8]
        rank = jnp.zeros((nb8, tq), F32)
        for j in range(n_blocks):
            sj = st[j:j + 1, :]
            beats = (sj > st) | ((sj == st) & (blk_t > j))
            rank = rank + jnp.where(beats, 1.0, 0.0)
        sel_t = jnp.where(rank < N_SEL, 1.0, 0.0)
        sel = jnp.concatenate([sel_t, jnp.zeros((LANES - nb8, tq), F32)], axis=0).T.astype(BF16)
        selx_ref[hf] = (_dot(sel, e_ref[...]) - 1.0) * (-NEG)

    n_win_tiles = WINDOW // tq + 1

    def bias_of(i, off, window):
        if off < 2:
            return near_ref[i, off]
        if window and off == n_win_tiles - 1:
            return far_ref[i] + edge_ref[...]
        return far_ref[i]

    def reset():
        mx_ref[...] = jnp.full_like(mx_ref, NEG)
        den_ref[...] = jnp.zeros_like(den_ref)
        acc_ref[...] = jnp.zeros_like(acc_ref)

    def score_tile(hf, kt_ref, start, slot, off, window):
        kt = kt_ref[0, :, pl.ds(start, tq)]
        z = _dot(qh_ref[hf], kt)
        for r in range(rep):
            zr = z[rows(r)] + bias_of(hf * rep + r, off, window)
            if not window:
                zr = zr + selx_ref[hf, :, pl.ds(start, tq)]
            zs_ref[r, :, pl.ds(slot, tq)] = zr
            mx_ref[r] = jnp.maximum(mx_ref[r], zr)

    def fix_max():
        for r in range(rep):
            mx_ref[r] = jnp.broadcast_to(jnp.max(mx_ref[r], axis=-1, keepdims=True), (tq, tq))

    def value_tile(vt_ref, start, slot):
        vt = vt_ref[0, :, pl.ds(start, tq)]
        ps = []
        for r in range(rep):
            p = jnp.exp(zs_ref[r, :, pl.ds(slot, tq)] - mx_ref[r])
            den_ref[r] += p
            ps.append(p.astype(BF16))
        acc_ref[...] += _dot_nt(jnp.concatenate(ps, axis=0), vt)

    def finish(hf, branch):
        for r in range(rep):
            i = hf * rep + r
            den = jnp.sum(den_ref[r], axis=-1, keepdims=True)
            out_ref[i] += (gate(branch, i) / den) * acc_ref[rows(r), :]

    for hf in range(2):
        def sel_score(j, off):
            start = pl.multiple_of(j * tq, tq)
            score_tile(hf, ks_ref, start, start, off, False)

        reset()
        sel_score(qi, 0)
        pl.when(qi >= 1)(lambda: sel_score(qi - 1, 1))

        def far_step(jj, carry):
            sel_score(qi - 2 - jj, 2)
            return carry

        lax.fori_loop(0, jnp.maximum(qi - 1, 0), far_step, 0)
        fix_max()

        def sel_value(j, carry):
            start = pl.multiple_of(j * tq, tq)
            value_tile(vs_ref, start, start)
            return carry

        lax.fori_loop(0, qi + 1, sel_value, 0)
        finish(hf, 1)

        def win_score(off):
            score_tile(hf, kw_ref, pl.multiple_of((qi - off) * tq, tq), off * tq, off, True)

        def win_value(off):
            value_tile(vw_ref, pl.multiple_of((qi - off) * tq, tq), off * tq)

        reset()
        win_score(0)
        for off in range(1, n_win_tiles):
            pl.when(qi >= off)(functools.partial(win_score, off))
        fix_max()
        win_value(0)
        for off in range(1, n_win_tiles):
            pl.when(qi >= off)(functools.partial(win_value, off))
        finish(hf, 2)

    for c in range(n_heads // 2):
        halves = []
        for i in (2 * c, 2 * c + 1):
            o = out_ref[i]
            if i % 2 != i // rep:
                o = pltpu.roll(o, HEAD_DIM, 1)
            halves.append(o)
        o_ref[:, c * LANES:(c + 1) * LANES] = jnp.where(lo_half, halves[0], halves[1]).astype(BF16)


def _nsa_prompt(q, kvc, selt, wint, gates, rel_bias, batch):
    m, d = q.shape
    t = m // batch
    tq = min(NSA_TQ, t)
    nq = t // tq
    nc = t // CMP_BLOCK
    nb = t // SEL_BLOCK
    rep = NSA_REP
    n_heads = 2 * rep
    assert t % tq == 0 and nb <= LANES and d == 2 * n_heads * HEAD_DIM and KV_W == 4 * LANES
    assert WINDOW % tq == 0 and WINDOW // tq >= 2 and tq % LANES == 0
    assert _bucket_saturates_from(tq + 1, t + WINDOW)
    n_win_tiles = WINDOW // tq + 1
    zs_width = max(t, n_win_tiles * tq)

    tl = np.arange(tq)
    dist0 = tl[:, None] - tl[None, :]
    near = jnp.stack([_bias_of(rel_bias, dist0) + jnp.where(dist0 >= 0, 0.0, NEG),
                      _bias_of(rel_bias, tq + dist0)], axis=1)
    far = _bias_of(rel_bias, np.full((1, tq), 2 * tq))
    edge = jnp.asarray(np.where((n_win_tiles - 1) * tq + dist0 <= WINDOW, 0.0, NEG), F32)
    shift = tq // CMP_BLOCK - 1
    n_near = shift + -(-(tq + CMP_BLOCK) // CMP_BLOCK) + 1
    assert tq % CMP_BLOCK == 0 and CMP_BLOCK * (n_near - shift) - CMP_BLOCK + 1 > tq
    assert n_near < LANES
    cb = _bias_of(rel_bias, CMP_BLOCK * (np.arange(LANES)[None, :] - shift) + tl[:, None] - (CMP_BLOCK - 1))
    expand = jnp.asarray(np.arange(LANES)[:, None] == (np.arange(t) // SEL_BLOCK)[None, :], BF16)
    pair = jnp.asarray((np.arange(nc) // (SEL_BLOCK // CMP_BLOCK))[:, None] == np.arange(LANES)[None, :], BF16)

    kvc_spec = lambda col0: pl.BlockSpec((nc, LANES), lambda b, gp, i: (b, col0 + gp))
    kvt_spec = lambda row0: pl.BlockSpec((1, LANES, t), lambda b, gp, i: (b, row0 + gp, 0))
    return pl.pallas_call(
        functools.partial(_nsa_prompt_body, n_near=n_near), grid=(batch, 2, nq),
        in_specs=[pl.BlockSpec((tq, n_heads * HEAD_DIM), lambda b, gp, i: (b * nq + i, gp)),
                  kvc_spec(0), kvc_spec(2), kvt_spec(0), kvt_spec(2), kvt_spec(0), kvt_spec(2),
                  pl.BlockSpec((tq, LANES), lambda b, gp, i: (b * nq + i, gp)),
                  pl.BlockSpec((n_heads, 2, tq, tq), lambda b, gp, i: (gp, 0, 0, 0)),
                  pl.BlockSpec((n_heads, 1, tq), lambda b, gp, i: (gp, 0, 0)),
                  pl.BlockSpec((tq, tq), lambda b, gp, i: (0, 0)),
                  pl.BlockSpec((n_heads, tq, LANES), lambda b, gp, i: (gp, 0, 0)),
                  pl.BlockSpec((LANES, t), lambda b, gp, i: (0, 0)),
                  pl.BlockSpec((nc, LANES), lambda b, gp, i: (0, 0))],
        out_specs=pl.BlockSpec((tq, n_heads * HEAD_DIM), lambda b, gp, i: (b * nq + i, gp)),
        out_shape=jax.ShapeDtypeStruct((m, d), BF16),
        scratch_shapes=[pltpu.VMEM((2, rep * tq, LANES), BF16), pltpu.VMEM((2, tq, t), F32),
                        pltpu.VMEM((rep, tq, zs_width), F32), pltpu.VMEM((rep, tq, tq), F32),
                        pltpu.VMEM((rep, tq, tq), F32), pltpu.VMEM((rep * tq, LANES), F32),
                        pltpu.VMEM((n_heads, tq, LANES), F32)],
        compiler_params=_params("parallel", "parallel", "arbitrary"), name="nsa_prompt")(
            q, kvc, kvc, selt, selt, wint, wint, gates, near, far, edge, cb, expand, pair)


def _nsa_select_body(q_ref, kvc_ref, cb_ref, pair_ref, oc_ref, idx_ref, *, past, n_take):
    kvc = kvc_ref[...]
    z = _dot_nt(q_ref[0], kvc[:, :HALF_W]) + cb_ref[...]
    e = jnp.exp(z - jnp.max(z, axis=-1, keepdims=True))
    p = e / jnp.maximum(jnp.sum(e, axis=-1, keepdims=True), 1e-30)
    oc = _dot(p.astype(BF16), kvc[:, HALF_W:])
    for g in range(NSA_GROUPS):
        oc_ref[0, g * NSA_REP:(g + 1) * NSA_REP, :] = oc[g * NSA_REP:(g + 1) * NSA_REP,
                                                         g * HEAD_DIM:(g + 1) * HEAD_DIM]
    imp = jnp.concatenate([jnp.sum(p[g * NSA_REP:(g + 1) * NSA_REP], axis=0, keepdims=True)
                           for g in range(NSA_GROUPS)], axis=0)
    imp2 = _dot_exact(imp, pair_ref[...])
    width = imp2.shape[1]
    lane = lax.broadcasted_iota(jnp.int32, (NSA_GROUPS, width), 1)
    lanef = lane.astype(F32)
    cur = past // SEL_BLOCK
    forced = (lane == 0) | (lane == cur) | (lane == cur - 1)
    score = jnp.where(lane * SEL_BLOCK > past, NEG, jnp.where(forced, SEL_FORCE, imp2))
    out_lane = lax.broadcasted_iota(jnp.int32, (NSA_GROUPS, LANES), 1)
    idx = jnp.zeros((NSA_GROUPS, LANES), F32)
    for k in range(n_take):
        best = jnp.max(score, axis=-1, keepdims=True)
        pick = jnp.min(jnp.where(score == best, lanef, float(width)), axis=-1, keepdims=True)
        idx = jnp.where(out_lane == k, pick, idx)
        score = jnp.where(lanef == pick, 2.0 * NEG, score)
    idx_ref[0] = idx.astype(jnp.int32)


def _nsa_select(q16, kvc, cb, past):
    bs, n_heads, _ = q16.shape
    n_cmp = kvc.shape[0] // bs
    n_blocks = past // SEL_BLOCK + 1
    width = -(-n_blocks // LANES) * LANES
    n_take = min(N_SEL, n_blocks)
    pair = jnp.asarray((np.arange(n_cmp) // (SEL_BLOCK // CMP_BLOCK))[:, None] == np.arange(width)[None, :], BF16)
    body = functools.partial(_nsa_select_body, past=past, n_take=n_take)
    return pl.pallas_call(
        body, grid=(bs,),
        in_specs=[pl.BlockSpec((1, n_heads, HALF_W), lambda b: (b, 0, 0)),
                  pl.BlockSpec((n_cmp, KV_W), lambda b: (b, 0)),
                  pl.BlockSpec((n_heads, n_cmp), lambda b: (0, 0)),
                  pl.BlockSpec((n_cmp, width), lambda b: (0, 0))],
        out_specs=[pl.BlockSpec((1, n_heads, HEAD_DIM), lambda b: (b, 0, 0)),
                   pl.BlockSpec((1, NSA_GROUPS, LANES), lambda b: (b, 0, 0))],
        out_shape=[jax.ShapeDtypeStruct((bs, n_heads, HEAD_DIM), F32),
                   jax.ShapeDtypeStruct((bs, NSA_GROUPS, LANES), jnp.int32)],
        compiler_params=_params("parallel"), name="nsa_select")(q16, kvc, cb, pair), n_take


BIAS_TABLE = 2 * LANES


def _lookup_bias(tab, dist):
    n = dist.shape[1]
    onehot = (lax.broadcasted_iota(jnp.int32, (BIAS_TABLE, n), 0) == jnp.minimum(dist, BIAS_TABLE - 1))
    return _dot_exact(tab, jnp.where(onehot, 1.0, 0.0).astype(BF16))


def _nsa_attend_body(pt_ref, ix_ref, q_ref, seln_ref, winn_ref, wcol_ref, win_ref, oc_ref, g_ref, tab_ref, wtab_ref,
                     pick_ref, *rest, past, n_take):
    pages = rest[:n_take]
    o_ref, wout_ref = rest[n_take:]
    b = pl.program_id(0)
    g = pl.program_id(1)
    n_past_blocks = past // SEL_BLOCK
    page = pages[0].shape[2]
    q4 = q_ref[0, 0]
    q4f = q4.astype(F32)
    tab = tab_ref[0]
    gates = g_ref[0, 0]

    has_new = False
    lane = lax.broadcasted_iota(jnp.int32, (1, page), 1)
    kpos, in_block = [], []
    for i in range(n_take):
        v = ix_ref[(b * NSA_GROUPS + g) * n_take + i]
        kp = (v * SEL_BLOCK) // page * page + lane
        kpos.append(kp)
        in_block.append(jnp.where((kp // SEL_BLOCK == v) & (v < n_past_blocks), 1, 0))
        has_new = jnp.logical_or(has_new, v >= n_past_blocks)
    kpos = jnp.concatenate(kpos, axis=1)
    dist = past - kpos
    ok = (jnp.concatenate(in_block, axis=1) == 1) & (dist >= 0)
    kv = jnp.concatenate([pages[i][0] for i in range(n_take)], axis=1).astype(BF16)
    z = jnp.where(ok, _dot(q4, kv) + _lookup_bias(tab, jnp.maximum(dist, 0)), NEG)
    new_s = _bf16_round(seln_ref[0])
    z_new = jnp.sum(q4f * new_s, axis=-1, keepdims=True) + tab[:, :1]
    z_new = jnp.where(has_new, z_new, NEG)
    m = jnp.maximum(jnp.max(z, axis=-1, keepdims=True), z_new)
    e = jnp.exp(z - m)
    e_new = jnp.exp(z_new - m)
    den = jnp.maximum(jnp.sum(e, axis=-1, keepdims=True) + e_new, 1e-30)
    o_s = _dot_nt((e / den).astype(BF16), kv) + _bf16_round(e_new / den) * new_s

    w = win_ref[0]
    wb = w.astype(BF16)
    new_w = _bf16_round(winn_ref[0])
    n_win = w.shape[1]
    zw = _dot(q4, wb) + wtab_ref[0][:, :n_win]
    zw_new = jnp.sum(q4f * new_w, axis=-1, keepdims=True) + wtab_ref[0][:, n_win:n_win + 1]
    mw = jnp.maximum(jnp.max(zw, axis=-1, keepdims=True), zw_new)
    ew = jnp.exp(zw - mw)
    ew_new = jnp.exp(zw_new - mw)
    denw = jnp.maximum(jnp.sum(ew, axis=-1, keepdims=True) + ew_new, 1e-30)
    o_w = _dot_nt((ew / denw).astype(BF16), wb) + _bf16_round(ew_new / denw) * new_w

    o_sw = gates[:, 1:2] * o_s + gates[:, 2:3] * o_w
    o_ref[0, 0] = gates[:, 0:1] * oc_ref[0, 0] + _dot_exact(o_sw, pick_ref[0])

    @pl.when(g == 0)
    def _():
        cols = lax.broadcasted_iota(jnp.int32, w.shape, 1)
        wout_ref[0] = jnp.where(cols == n_win - 1, wcol_ref[0], pltpu.roll(w, n_win - 1, 1))


def _nsa_attend(q4, sel_new, win_new, state_win, layer, cache_sel, page_table, idx, n_take, oc, gates, rel_bias):
    bs = q4.shape[0]
    n_pool, page = cache_sel.shape[1:3]
    n_pages = page_table.shape[1]
    past = n_pages * page
    n_past_blocks = past // SEL_BLOCK
    n_win = state_win.shape[2]
    assert _bucket_saturates_from(BIAS_TABLE - 1, past + SEL_BLOCK) and page % SEL_BLOCK == 0
    pages = _position_minor_pages(cache_sel)
    base = layer * n_pool
    win_t = jnp.transpose(state_win, (0, 1, 3, 4, 5, 2)).reshape(state_win.shape[0] * bs, KV_W, n_win)
    tab = _bias_of(rel_bias, np.arange(BIAS_TABLE)).reshape(NSA_GROUPS, NSA_REP, BIAS_TABLE)
    wpad = -(-(n_win + 1) // LANES) * LANES
    wdist = np.maximum(n_win - np.arange(wpad), 0)
    wtab = _bias_of(rel_bias, wdist).reshape(NSA_GROUPS, NSA_REP, wpad)
    feats = np.arange(KV_W)[None, :, None]
    pick = jnp.asarray(feats == HALF_W + np.arange(NSA_GROUPS)[:, None, None] * HEAD_DIM
                       + np.arange(HEAD_DIM)[None, None, :], BF16)

    def page_spec(i):
        def index(b, g, pt, ix):
            v = jnp.minimum(ix[(b * NSA_GROUPS + g) * n_take + i], n_past_blocks - 1)
            return (base + pt[b, (v * SEL_BLOCK) // page], 0, 0)
        return pl.BlockSpec((1, KV_W, page), index)

    grid_spec = pltpu.PrefetchScalarGridSpec(
        num_scalar_prefetch=2, grid=(bs, NSA_GROUPS),
        in_specs=[pl.BlockSpec((1, 1, NSA_REP, KV_W), lambda b, g, pt, ix: (b, g, 0, 0)),
                  pl.BlockSpec((1, 1, KV_W), lambda b, g, pt, ix: (b, 0, 0)),
                  pl.BlockSpec((1, 1, KV_W), lambda b, g, pt, ix: (b, 0, 0)),
                  pl.BlockSpec((1, KV_W, 1), lambda b, g, pt, ix: (b, 0, 0)),
                  pl.BlockSpec((1, KV_W, n_win), lambda b, g, pt, ix: (layer * bs + b, 0, 0)),
                  pl.BlockSpec((1, 1, NSA_REP, HEAD_DIM), lambda b, g, pt, ix: (b, g, 0, 0)),
                  pl.BlockSpec((1, 1, NSA_REP, LANES), lambda b, g, pt, ix: (b, g, 0, 0)),
                  pl.BlockSpec((1, NSA_REP, BIAS_TABLE), lambda b, g, pt, ix: (g, 0, 0)),
                  pl.BlockSpec((1, NSA_REP, wpad), lambda b, g, pt, ix: (g, 0, 0)),
                  pl.BlockSpec((1, KV_W, HEAD_DIM), lambda b, g, pt, ix: (g, 0, 0))]
                 + [page_spec(i) for i in range(n_take)],
        out_specs=[pl.BlockSpec((1, 1, NSA_REP, HEAD_DIM), lambda b, g, pt, ix: (b, g, 0, 0)),
                   pl.BlockSpec((1, KV_W, n_win), lambda b, g, pt, ix: (b, 0, 0))])
    body = functools.partial(_nsa_attend_body, past=past, n_take=n_take)
    return pl.pallas_call(
        body, grid_spec=grid_spec,
        out_shape=[jax.ShapeDtypeStruct((bs, NSA_GROUPS, NSA_REP, HEAD_DIM), F32),
                   jax.ShapeDtypeStruct((bs, KV_W, n_win), F32)],
        compiler_params=_params("parallel", "arbitrary"), name="nsa_attend")(
            page_table, idx, q4, sel_new, win_new, win_new.reshape(bs, KV_W, 1), win_t, oc, gates, tab, wtab, pick,
            *([pages] * n_take))


def _compress_weights(cmp_pos, w_cmp):
    pos = jnp.broadcast_to(jnp.swapaxes(cmp_pos, 0, 1)[:, :, None, :], (CMP_BLOCK, 2, NSA_GROUPS, HEAD_DIM))
    eye = jnp.eye(NSA_GROUPS, dtype=w_cmp.dtype)
    w2 = jnp.einsum('clde,gh->lcgdhe', w_cmp, eye).reshape(CMP_BLOCK, 2, HALF_W, HALF_W)
    return pos.reshape(CMP_BLOCK, KV_W), w2.astype(BF16)


def _nsa_gate_weights(wg, paired):
    d = wg.shape[0]
    if paired:
        w = wg.reshape(d, 3, 2, 2 * NSA_REP).transpose(0, 2, 1, 3).reshape(d, 2, 3 * 2 * NSA_REP)
        w = jnp.pad(w, ((0, 0), (0, 0), (0, LANES - w.shape[2])))
        return w.reshape(d, 2 * LANES).astype(BF16)
    return jnp.pad(wg, ((0, 0), (0, LANES - wg.shape[1]))).astype(BF16)


def _rows_by_position(x_t, lead):
    batch, _, t = x_t.shape
    nd = len(lead)
    return jnp.transpose(x_t.reshape(batch, *lead, t), (0, nd + 1) + tuple(range(1, nd + 1)))


def kernel(x_prompt, x_sample, cache_sb_kv, cache_cmp_kv, cache_sel_kv, state_win_kv, page_table,
           w_in_sb, w_out_sb, w_in_nsa, w_out_nsa, cmp_pos, w_cmp, rel_bias, ln_g, ln_b, w_up, w_down):
    bp, t, d = x_prompt.shape
    bs, tn, _ = x_sample.shape
    assert tn == 1 and d == NSA_GROUPS * NSA_REP * HEAD_DIM
    depth = ln_g.shape[0]
    alpha = (2 * depth) ** 0.25
    past = page_table.shape[1] * cache_sb_kv.shape[2]
    n_heads = d // HEAD_DIM
    sb_lead = (2, n_heads, HEAD_DIM)
    nsa_lead = (2, NSA_GROUPS, HEAD_DIM)
    xp = x_prompt.reshape(bp * t, d)
    xs = x_sample.reshape(bs, d)
    sb_p, sb_s, cmp_p, cmp_s, sel_p, sel_s, win_p, win_s = [], [], [], [], [], [], [], []
    for i in range(depth):
        l = i // 2
        if i % 2 == 0:
            w = w_in_sb[l].astype(BF16)
            wq, wkv = w[:, :d], w[:, d:]
            q, v, kvt, kt = _sb_proj(xp, wq, w[:, 2 * d:], wkv.T, bp)
            mp = _sb_prompt(q, kt, v, bp)
            qs, kvs = _sb_proj_rows(xs, wq, wkv)
            ms = _sb_sample(qs, cache_sb_kv, l, page_table)
            sb_p.append(_rows_by_position(kvt, sb_lead))
            sb_s.append(kvs.reshape(bs, tn, *sb_lead))
            w_out = w_out_sb[l]
        else:
            w = w_in_nsa[l].astype(BF16)
            wq = w[:, :d]
            wkv = w[:, d:d + 3 * KV_W]
            wg = w_in_nsa[l][:, d + 3 * KV_W:]
            pos_rows, w2 = _compress_weights(cmp_pos[l], w_cmp[l])
            q, c_rows, gates, c_t, s_t, w_t, s_tb, w_tb = _nsa_proj(
                xp, wq, wkv[:, :KV_W], _nsa_gate_weights(wg, True), wkv.T, bp)
            kvc = _compress(c_rows.reshape(bp * t // CMP_BLOCK, CMP_BLOCK * KV_W), pos_rows.reshape(1, -1), w2)
            mp = _nsa_prompt(q, kvc, s_tb, w_tb, gates, rel_bias, bp)
            w_buf = min(WINDOW, t)
            cmp_p.append(_rows_by_position(c_t, nsa_lead))
            sel_p.append(_rows_by_position(s_t, nsa_lead))
            win_p.append(_rows_by_position(w_t[:, :, t - w_buf:], nsa_lead))
            qs, kv_s, gates_s = _nsa_proj_rows(xs, wq, wkv, _nsa_gate_weights(wg, False))
            c_s, s_s, w_s = kv_s[:, :KV_W], kv_s[:, KV_W:2 * KV_W], kv_s[:, 2 * KV_W:]
            kvc_s = _compress_paged(cache_cmp_kv, l, page_table, pos_rows, w2)
            qh = qs.reshape(bs, NSA_GROUPS, NSA_REP, 1, HEAD_DIM)
            eye = jnp.eye(NSA_GROUPS, dtype=qs.dtype)[None, :, None, :, None]
            q16 = (qh * eye).reshape(bs, n_heads, HALF_W)
            q4 = jnp.pad(q16, ((0, 0), (0, 0), (0, HALF_W))).reshape(bs, NSA_GROUPS, NSA_REP, KV_W)
            n_cmp = past // CMP_BLOCK
            cb_s = _bias_of(rel_bias, past - (np.arange(n_cmp) * CMP_BLOCK + CMP_BLOCK - 1))
            (oc, idx), n_take = _nsa_select(q16, kvc_s, cb_s, past)
            g3 = gates_s[:, :3 * n_heads].reshape(bs, 3, NSA_GROUPS, NSA_REP).transpose(0, 2, 3, 1)
            g3 = jnp.pad(g3, ((0, 0), (0, 0), (0, 0), (0, LANES - 3)))
            o4, wnew = _nsa_attend(q4, s_s.reshape(bs, 1, KV_W), w_s.reshape(bs, 1, KV_W), state_win_kv, l,
                                   cache_sel_kv, page_table, idx[:, :, :n_take].reshape(-1), n_take,
                                   oc.reshape(bs, NSA_GROUPS, NSA_REP, HEAD_DIM), g3, rel_bias)
            ms = o4.reshape(bs, d)
            cmp_s.append(c_s.reshape(bs, tn, *nsa_lead))
            sel_s.append(s_s.reshape(bs, tn, *nsa_lead))
            win_s.append(_rows_by_position(wnew, nsa_lead))
            w_out = w_out_nsa[l]
        tail_args = (w_out.astype(BF16), ln_g[i, 0][None], ln_b[i, 0][None], w_up[i].astype(BF16),
                     w_down[i].astype(BF16), ln_g[i, 1][None], ln_b[i, 1][None], alpha)
        xp = _tail(xp, mp, *tail_args)
        xs = _tail(xs, ms, *tail_args)
    return (xp.reshape(bp, t, d), xs.reshape(bs, tn, d), jnp.stack(sb_p), jnp.stack(sb_s), jnp.stack(cmp_p),
            jnp.stack(cmp_s), jnp.stack(sel_p), jnp.stack(sel_s), jnp.stack(win_p), jnp.stack(win_s))
```

```python
import functools
import math

import numpy as np
import jax
import jax.numpy as jnp
from jax import lax
from jax.experimental import pallas as pl
from jax.experimental.pallas import tpu as pltpu

F32 = jnp.float32
BF16 = jnp.bfloat16

HEAD_DIM = 64
NSA_GROUPS = 4
NSA_REP = 4
CMP_BLOCK = 32
SEL_BLOCK = 64
N_SEL = 16
WINDOW = 512
N_BUCKETS = 32
MAX_DISTANCE = 128
LN_EPS = 1e-5
SEL_FORCE = 1e4
NEG = -1e30

LANES = 128
ROW_TILE = 256
TAIL_TILE = 512
SB_TILE = 256
NSA_TQ = 256
SB_PAGES_PER_STEP = 16
CMP_PAGES_PER_STEP = 16
CMP_DOT_ROWS = 256
SELECT_SEQS = 8
VMEM_LIMIT = 56 * 1024 * 1024

KV_W = 2 * NSA_GROUPS * HEAD_DIM
HALF_W = NSA_GROUPS * HEAD_DIM

_NT = (((1,), (1,)), ((), ()))


def _dot(a, b):
    return jnp.dot(a, b, preferred_element_type=F32)


def _dot_nt(a, b):
    return lax.dot_general(a, b, _NT, preferred_element_type=F32)


def _split2(x):
    hi = x.astype(BF16)
    lo = (x - hi.astype(F32)).astype(BF16)
    return hi, lo


def _split3(x):
    hi = x.astype(BF16)
    r = x - hi.astype(F32)
    mid = r.astype(BF16)
    lo = (r - mid.astype(F32)).astype(BF16)
    return hi, mid, lo


def _dot_exact(x, m):
    hi, mid, lo = _split3(x)
    return _dot(hi, m) + _dot(mid, m) + _dot(lo, m)


def _softplus(z):
    return jnp.maximum(z, 0.0) + jnp.log(1.0 + jnp.exp2(jnp.abs(z) * (-1.0 / math.log(2.0))))


def _layer_norm(y, g, b):
    mu = jnp.mean(y, axis=-1, keepdims=True)
    yc = y - mu
    var = jnp.mean(yc * yc, axis=-1, keepdims=True)
    return yc * lax.rsqrt(var + LN_EPS) * g + b


def _bf16_round(x):
    return x.astype(BF16).astype(F32)


def _params(*sem):
    return pltpu.CompilerParams(dimension_semantics=sem, vmem_limit_bytes=VMEM_LIMIT)


def _const_spec(a):
    nd = a.ndim
    return pl.BlockSpec(a.shape, lambda *_: (0,) * nd, pipeline_mode=pl.Buffered(1))


def _row_call(body, row_args, const_args, outs, name, t_outs=(), seq_len=None, row_tile=ROW_TILE):
    m = row_args[0].shape[0]
    tm = min(row_tile, m)
    assert m % tm == 0
    in_specs = [pl.BlockSpec((tm, a.shape[1]), lambda i: (i, 0)) for a in row_args]
    in_specs += [_const_spec(a) for a in const_args]
    out_specs = [pl.BlockSpec((tm, o.shape[1]), lambda i: (i, 0)) for o in outs]
    if t_outs:
        nt = seq_len // tm
        assert seq_len % tm == 0
        out_specs += [pl.BlockSpec((1, o.shape[1], tm), lambda i: (i // nt, 0, i % nt)) for o in t_outs]
    return pl.pallas_call(
        body, grid=(m // tm,), in_specs=in_specs, out_specs=out_specs, out_shape=list(outs) + list(t_outs),
        compiler_params=_params("parallel"), name=name)(*row_args, *const_args)


def _sb_proj_rows_body(x_ref, wq_ref, wkv_ref, q_ref, kv_ref):
    xb = x_ref[...].astype(BF16)
    q_ref[...] = (_dot(xb, wq_ref[...]) * HEAD_DIM ** -0.5).astype(BF16)
    kv_ref[...] = _dot(xb, wkv_ref[...])


def _sb_proj_rows(x, wq, wkv):
    m, d = x.shape
    outs = [jax.ShapeDtypeStruct((m, d), BF16), jax.ShapeDtypeStruct((m, 2 * d), F32)]
    return _row_call(_sb_proj_rows_body, [x], [wq, wkv], outs, "sb_proj_rows")


def _sb_proj_body(x_ref, wq_ref, wv_ref, wkvt_ref, q_ref, v_ref, kvt_ref, kt_ref):
    xb = x_ref[...].astype(BF16)
    q_ref[...] = (_dot(xb, wq_ref[...]) * HEAD_DIM ** -0.5).astype(BF16)
    v_ref[...] = _dot(xb, wv_ref[...]).astype(BF16)
    kvt = _dot_nt(wkvt_ref[...], xb)
    kvt_ref[0] = kvt
    kt_ref[0] = kvt[:kt_ref.shape[1]].astype(BF16)


def _sb_proj(x, wq, wv, wkvt, batch):
    m, d = x.shape
    t = m // batch
    outs = [jax.ShapeDtypeStruct((m, d), BF16), jax.ShapeDtypeStruct((m, d), BF16)]
    t_outs = [jax.ShapeDtypeStruct((batch, 2 * d, t), F32), jax.ShapeDtypeStruct((batch, d, t), BF16)]
    return _row_call(_sb_proj_body, [x], [wq, wv, wkvt], outs, "sb_proj", t_outs, t)


def _tail_body(x_ref, m_ref, wo_ref, g1_ref, b1_ref, wu_ref, wd_ref, g2_ref, b2_ref, o_ref, *, alpha):
    y = alpha * x_ref[...] + _dot(m_ref[...].astype(BF16), wo_ref[...])
    y = _layer_norm(y, g1_ref[...], b1_ref[...])
    h = jnp.maximum(_dot(y.astype(BF16), wu_ref[...]), 0.0)
    y = alpha * y + _dot((h * h).astype(BF16), wd_ref[...])
    o_ref[...] = _layer_norm(y, g2_ref[...], b2_ref[...])


def _tail(x, mix, wo, g1, b1, wu, wd, g2, b2, alpha):
    outs = [jax.ShapeDtypeStruct(x.shape, F32)]
    body = functools.partial(_tail_body, alpha=alpha)
    return _row_call(body, [x, mix], [wo, g1, b1, wu, wd, g2, b2], outs, "block_tail", row_tile=TAIL_TILE)[0]


def _rev_cumsum_matrix(n):
    j = np.arange(n)
    u = j[:, None] >= j[None, :]
    return jnp.asarray(np.concatenate([u, u], axis=0), BF16)


def _sb_local(z, u, mask):
    sp = _softplus(z)
    if mask is not None:
        sp = jnp.where(mask, sp, 0.0)
    cum = _dot(jnp.concatenate(_split2(sp), axis=1), u)
    return z - cum, cum[:, :1]


def _sb_weights(d, suf, mask):
    a = jnp.exp(d - suf)
    if mask is not None:
        a = jnp.where(mask, a, 0.0)
    return a.astype(BF16)


def _sb_prompt_body(q_ref, kt_ref, v_ref, u_ref, o_ref, acc_ref):
    tq = q_ref.shape[0]
    qi = pl.program_id(2)
    lane = lax.broadcasted_iota(jnp.int32, (tq, LANES), 1)
    lo_half = lane < HEAD_DIM
    q2 = q_ref[...]
    zero = jnp.zeros_like(q2)
    qs = (jnp.where(lo_half, q2, zero), jnp.where(lo_half, zero, q2))
    u = u_ref[...]
    row = lax.broadcasted_iota(jnp.int32, (tq, tq), 0)
    col = lax.broadcasted_iota(jnp.int32, (tq, tq), 1)
    strictly_before = col < row
    acc_ref[...] = jnp.zeros_like(acc_ref)

    def tiles(js, sufs, mask):
        local = []
        for j in js:
            start = pl.multiple_of(j * tq, tq)
            kt2 = kt_ref[0, :, pl.ds(start, tq)]
            v2 = v_ref[pl.ds(start, tq), :]
            local.append((v2, [_sb_local(_dot(qs[h], kt2), u, mask) for h in range(2)]))
        sufs = list(sufs)
        for v2, per_head in local:
            for h, (d, tot) in enumerate(per_head):
                acc_ref[h] += _dot(_sb_weights(d, sufs[h], mask), v2)
                sufs[h] = sufs[h] + tot
        return tuple(sufs)

    zero_suf = jnp.zeros((tq, 1), F32)
    sufs = tiles([qi], (zero_suf, zero_suf), strictly_before)
    sufs = lax.fori_loop(0, qi // 2, lambda jj, s: tiles([qi - 1 - 2 * jj, qi - 2 - 2 * jj], s, None), sufs)

    @pl.when(qi % 2 == 1)
    def _():
        tiles([0], sufs, None)

    o_ref[...] = jnp.where(lo_half, acc_ref[0], acc_ref[1]).astype(BF16)


def _sb_prompt(q, kt, v, batch):
    m, d = q.shape
    t = m // batch
    tq = min(SB_TILE, t)
    assert t % tq == 0 and d % LANES == 0
    nq = t // tq
    u = _rev_cumsum_matrix(tq)
    return pl.pallas_call(
        _sb_prompt_body, grid=(batch, d // LANES, nq),
        in_specs=[pl.BlockSpec((tq, LANES), lambda b, h, i: (b * nq + i, h)),
                  pl.BlockSpec((1, LANES, t), lambda b, h, i: (b, h, 0)),
                  pl.BlockSpec((t, LANES), lambda b, h, i: (b, h)),
                  pl.BlockSpec((2 * tq, tq), lambda b, h, i: (0, 0))],
        out_specs=pl.BlockSpec((tq, LANES), lambda b, h, i: (b * nq + i, h)),
        out_shape=jax.ShapeDtypeStruct((m, d), BF16),
        scratch_shapes=[pltpu.VMEM((2, tq, LANES), F32)],
        compiler_params=_params("parallel", "parallel", "arbitrary"), name="sb_prompt")(q, kt, v, u)


def _position_minor_pages(cache):
    n_layers, n_pool, page = cache.shape[:3]
    feat = cache.shape[3] * cache.shape[4] * cache.shape[5]
    return jnp.transpose(cache, (0, 1, 3, 4, 5, 2)).reshape(n_layers * n_pool, feat, page)


def _sb_sample_body(pt_ref, q_ref, u_ref, *rest, pages_per_step):
    pages = rest[:pages_per_step]
    o_ref, acc_ref, suf_ref = rest[pages_per_step:]
    s = pl.program_id(1)
    d = q_ref.shape[-1]
    heads = d // HEAD_DIM

    @pl.when(s == 0)
    def _():
        acc_ref[...] = jnp.zeros_like(acc_ref)
        suf_ref[...] = jnp.zeros_like(suf_ref)

    own = (lax.broadcasted_iota(jnp.int32, (heads, d), 1) // HEAD_DIM
           == lax.broadcasted_iota(jnp.int32, (heads, d), 0))
    qrow = jnp.broadcast_to(q_ref[0].astype(F32), (heads, d))
    qbd = jnp.where(own, qrow, 0.0).astype(BF16)
    u = u_ref[...]
    acc = acc_ref[...]
    suf = suf_ref[...]
    local = []
    for p in range(pages_per_step):
        kt = pages[p][0, :d, :].astype(BF16)
        local.append(_sb_local(_dot(qbd, kt), u, None))
    for p, (dl, tot) in enumerate(local):
        vt = pages[p][0, d:, :].astype(BF16)
        acc = acc + _dot_nt(_sb_weights(dl, suf, None), vt)
        suf = suf + tot
    acc_ref[...] = acc
    suf_ref[...] = suf

    @pl.when(s == pl.num_programs(1) - 1)
    def _():
        o_ref[0] = jnp.sum(jnp.where(own, acc, 0.0), axis=0, keepdims=True)


def _sb_sample(q, cache, layer, page_table):
    b, d = q.shape
    n_pool, page = cache.shape[1:3]
    assert cache.shape[3] * cache.shape[4] * cache.shape[5] == 2 * d
    n_pages = page_table.shape[1]
    pps = math.gcd(SB_PAGES_PER_STEP, n_pages)
    pages = _position_minor_pages(cache)
    base = layer * n_pool

    def page_spec(p):
        return pl.BlockSpec((1, 2 * d, page),
                            lambda i, s, pt: (base + pt[i, n_pages - 1 - (s * pps + p)], 0, 0))

    grid_spec = pltpu.PrefetchScalarGridSpec(
        num_scalar_prefetch=1, grid=(b, n_pages // pps),
        in_specs=[pl.BlockSpec((1, 1, d), lambda i, s, pt: (i, 0, 0)),
                  pl.BlockSpec((2 * page, page), lambda i, s, pt: (0, 0))]
                 + [page_spec(p) for p in range(pps)],
        out_specs=pl.BlockSpec((1, 1, d), lambda i, s, pt: (i, 0, 0)),
        scratch_shapes=[pltpu.VMEM((d // HEAD_DIM, d), F32), pltpu.VMEM((d // HEAD_DIM, 1), F32)])
    out = pl.pallas_call(
        functools.partial(_sb_sample_body, pages_per_step=pps), grid_spec=grid_spec,
        out_shape=jax.ShapeDtypeStruct((b, 1, d), F32),
        compiler_params=_params("parallel", "arbitrary"), name="sb_sample")(
            page_table, q.reshape(b, 1, d), _rev_cumsum_matrix(page), *([pages] * pps))
    return out.reshape(b, d)


def _nsa_proj_rows_body(x_ref, wq_ref, wkv_ref, wg_ref, q_ref, kv_ref, g_ref):
    xb = x_ref[...].astype(BF16)
    q_ref[...] = (_dot(xb, wq_ref[...]) * HEAD_DIM ** -0.5).astype(BF16)
    kv_ref[...] = _dot(xb, wkv_ref[...])
    g_ref[...] = jax.nn.sigmoid(_dot(xb, wg_ref[...]))


def _nsa_proj_rows(x, wq, wkv, wg):
    m, d = x.shape
    outs = [jax.ShapeDtypeStruct((m, d), BF16), jax.ShapeDtypeStruct((m, wkv.shape[1]), F32),
            jax.ShapeDtypeStruct((m, wg.shape[1]), F32)]
    return _row_call(_nsa_proj_rows_body, [x], [wq, wkv, wg], outs, "nsa_proj_rows")


def _nsa_proj_body(x_ref, wq_ref, wc_ref, wg_ref, wkvt_ref, q_ref, cmp_ref, g_ref,
                   cmpt_ref, selt_ref, wint_ref, seltb_ref, wintb_ref):
    xb = x_ref[...].astype(BF16)
    q_ref[...] = (_dot(xb, wq_ref[...]) * HEAD_DIM ** -0.5).astype(BF16)
    cmp_ref[...] = _dot(xb, wc_ref[...])
    g_ref[...] = jax.nn.sigmoid(_dot(xb, wg_ref[...]))
    kvt = _dot_nt(wkvt_ref[...], xb)
    cmpt_ref[0] = kvt[:KV_W]
    sel = kvt[KV_W:2 * KV_W]
    win = kvt[2 * KV_W:]
    selt_ref[0] = sel
    wint_ref[0] = win
    seltb_ref[0] = sel.astype(BF16)
    wintb_ref[0] = win.astype(BF16)


def _nsa_proj(x, wq, wc, wg, wkvt, batch):
    m, d = x.shape
    t = m // batch
    outs = [jax.ShapeDtypeStruct((m, d), BF16), jax.ShapeDtypeStruct((m, KV_W), F32),
            jax.ShapeDtypeStruct((m, wg.shape[1]), F32)]
    t_outs = [jax.ShapeDtypeStruct((batch, KV_W, t), F32)] * 3 + [jax.ShapeDtypeStruct((batch, KV_W, t), BF16)] * 2
    return _row_call(_nsa_proj_body, [x], [wq, wc, wg, wkvt], outs, "nsa_proj", t_outs, t)


CMP_L_CHUNK = 4


def _compress_body(x_ref, pos_ref, w_ref, o_ref, acc_ref):
    kc = pl.program_id(1)

    @pl.when(kc == 0)
    def _():
        acc_ref[...] = jnp.zeros_like(acc_ref)

    xb = (x_ref[...] + pos_ref[...]).astype(BF16)
    for l in range(CMP_L_CHUNK):
        for c in range(2):
            lo = l * KV_W + c * HALF_W
            acc_ref[:, c * HALF_W:(c + 1) * HALF_W] += _dot(xb[:, lo:lo + HALF_W], w_ref[l, c])

    @pl.when(kc == pl.num_programs(1) - 1)
    def _():
        o_ref[...] = acc_ref[...].astype(BF16)


def _compress(x2, pos_flat, w2):
    rows, width = x2.shape
    tm = min(512, rows)
    assert rows % tm == 0 and CMP_BLOCK % CMP_L_CHUNK == 0
    kw = CMP_L_CHUNK * KV_W
    return pl.pallas_call(
        _compress_body, grid=(rows // tm, width // kw),
        in_specs=[pl.BlockSpec((tm, kw), lambda i, k: (i, k)),
                  pl.BlockSpec((1, kw), lambda i, k: (0, k)),
                  pl.BlockSpec((CMP_L_CHUNK, 2, HALF_W, HALF_W), lambda i, k: (k, 0, 0, 0))],
        out_specs=pl.BlockSpec((tm, KV_W), lambda i, k: (i, 0)),
        out_shape=jax.ShapeDtypeStruct((rows, KV_W), BF16),
        scratch_shapes=[pltpu.VMEM((tm, KV_W), F32)],
        compiler_params=_params("parallel", "arbitrary"), name="nsa_compress")(x2, pos_flat, w2)


def _compress_paged_body(pt_ref, pos_ref, perm_ref, w_ref, *rest, pages_per_step, steps_per_dot):
    pages = rest[:pages_per_step]
    o_ref, x_ref = rest[pages_per_step:]
    s = pl.program_id(1)
    slot = s % steps_per_dot
    pos = pos_ref[...]
    group = perm_ref.shape[0] // pages[0].shape[2]
    slab = perm_ref.shape[0] // CMP_BLOCK
    r0 = pl.multiple_of(slot * (pages_per_step // group) * slab, slab)
    for pg in range(pages_per_step // group):
        a = jnp.concatenate([pages[pg * group + k][0] + pos for k in range(group)], axis=1).astype(BF16)
        xp = _dot_nt(perm_ref[...], a)
        for l in range(CMP_BLOCK):
            x_ref[l, pl.ds(r0 + pg * slab, slab), :] = xp[l * slab:(l + 1) * slab, :]

    @pl.when(slot == steps_per_dot - 1)
    def _():
        for c in range(2):
            xc = jnp.concatenate([x_ref[l, :, c * HALF_W:(c + 1) * HALF_W].astype(BF16)
                                  for l in range(CMP_BLOCK)], axis=1)
            o_ref[:, c * HALF_W:(c + 1) * HALF_W] = _dot(xc, w_ref[c]).astype(BF16)


def _compress_paged(cache, layer, page_table, pos_rows, w2):
    n_pool, page = cache.shape[1:3]
    bs, n_pages = page_table.shape
    rpp = page // CMP_BLOCK
    pages = _position_minor_pages(cache)
    base = layer * n_pool
    pps = math.gcd(CMP_PAGES_PER_STEP, n_pages)
    spd = math.gcd(max(1, CMP_DOT_ROWS // (pps * rpp)), n_pages // pps)
    rows = pps * spd * rpp
    n_steps = n_pages // pps
    group = 8 // rpp
    assert 8 % rpp == 0 and pps % group == 0
    pos_t = jnp.tile(pos_rows, (rpp, 1)).T
    l_, k_, n_ = np.meshgrid(np.arange(CMP_BLOCK), np.arange(group), np.arange(rpp), indexing="ij")
    perm = np.zeros((group * page, group * page), np.float32)
    perm[(l_ * group * rpp + k_ * rpp + n_).ravel(), (k_ * page + n_ * CMP_BLOCK + l_).ravel()] = 1.0

    def page_spec(p):
        return pl.BlockSpec((1, KV_W, page), lambda i, s, pt: (base + pt[i, s * pps + p], 0, 0))

    grid_spec = pltpu.PrefetchScalarGridSpec(
        num_scalar_prefetch=1, grid=(bs, n_steps),
        in_specs=[pl.BlockSpec((KV_W, page), lambda i, s, pt: (0, 0)),
                  pl.BlockSpec(perm.shape, lambda i, s, pt: (0, 0)),
                  pl.BlockSpec((2, CMP_BLOCK * HALF_W, HALF_W), lambda i, s, pt: (0, 0, 0))]
                 + [page_spec(p) for p in range(pps)],
        out_specs=pl.BlockSpec((rows, KV_W), lambda i, s, pt: (i * (n_steps // spd) + s // spd, 0)),
        scratch_shapes=[pltpu.VMEM((CMP_BLOCK, rows, KV_W), F32)])
    body = functools.partial(_compress_paged_body, pages_per_step=pps, steps_per_dot=spd)
    return pl.pallas_call(
        body, grid_spec=grid_spec, out_shape=jax.ShapeDtypeStruct((bs * n_pages * rpp, KV_W), BF16),
        compiler_params=_params("parallel", "arbitrary"), name="nsa_compress_paged")(
            page_table, pos_t, jnp.asarray(perm, BF16),
            jnp.swapaxes(w2, 0, 1).reshape(2, CMP_BLOCK * HALF_W, HALF_W), *([pages] * pps))


def _rel_bucket(dist):
    dist = jnp.maximum(dist, 0)
    max_exact = N_BUCKETS // 2
    dd = jnp.maximum(dist, 1).astype(F32)
    large = max_exact + (jnp.log(dd / max_exact) / math.log(MAX_DISTANCE / max_exact)
                         * (N_BUCKETS - max_exact)).astype(jnp.int32)
    large = jnp.minimum(large, N_BUCKETS - 1)
    return jnp.where(dist < max_exact, dist, large)


def _bias_of(rel_bias, dist):
    onehot = jax.nn.one_hot(_rel_bucket(jnp.asarray(dist, jnp.int32)), N_BUCKETS, dtype=F32)
    return jnp.einsum('...k,kh->h...', onehot, rel_bias.astype(F32), precision=lax.Precision.HIGHEST)


def _bucket_saturates_from(d0, d1):
    dd = np.arange(d0, d1 + 1).astype(np.float32)
    max_exact = N_BUCKETS // 2
    large = max_exact + (np.log(dd / np.float32(max_exact)) / np.float32(math.log(MAX_DISTANCE / max_exact))
                         * np.float32(N_BUCKETS - max_exact)).astype(np.int32)
    return bool(np.all(large >= N_BUCKETS - 1)) and d0 >= max_exact


def _nsa_prompt_body(q_ref, kc_ref, vc_ref, ks_ref, vs_ref, kw_ref, vw_ref, g_ref, near_ref, far_ref, edge_ref,
                     cb_ref, e_ref, pair_ref, o_ref, qh_ref, selx_ref, zs_ref, mx_ref, den_ref, acc_ref, out_ref,
                     *, n_near):
    tq = q_ref.shape[0]
    nc = kc_ref.shape[0]
    rep = NSA_REP
    n_heads = 2 * rep
    qi = pl.program_id(2)
    lane = lax.broadcasted_iota(jnp.int32, (tq, LANES), 1)
    rowl = lax.broadcasted_iota(jnp.int32, (tq, LANES), 0)
    lo_half = lane < HEAD_DIM
    half_mask = (lo_half, jnp.logical_not(lo_half))
    gates = g_ref[...]

    def gate(branch, i):
        k = branch * n_heads + i
        return gates[:, k:k + 1]

    def rows(r):
        return slice(r * tq, (r + 1) * tq)

    for i in range(n_heads):
        hf, r = divmod(i, rep)
        qf = q_ref[:, (i // 2) * LANES:(i // 2 + 1) * LANES].astype(F32)
        if i % 2 != hf:
            qf = pltpu.roll(qf, HEAD_DIM, 1)
        qh_ref[hf, rows(r), :] = jnp.where(half_mask[hf], qf, 0.0).astype(BF16)

    coln = lax.broadcasted_iota(jnp.int32, (tq, nc), 1)
    rown = lax.broadcasted_iota(jnp.int32, (tq, nc), 0)
    valid_c = qi * tq + rown - (coln * CMP_BLOCK + CMP_BLOCK - 1) >= 0
    m_rel = jnp.clip((qi + 1) * (tq // CMP_BLOCK) - 1 - coln[:1], 0, n_near)
    pick_col = jnp.where(lax.broadcasted_iota(jnp.int32, (LANES, nc), 0) == m_rel, 1.0, 0.0).astype(BF16)
    kc = kc_ref[...]
    vc = vc_ref[...]
    n_blocks = ks_ref.shape[2] // SEL_BLOCK
    nb8 = -(-n_blocks // 8) * 8
    blk_t = lax.broadcasted_iota(jnp.int32, (nb8, tq), 0)
    for hf in range(2):
        imp = jnp.zeros((tq, nc), F32)
        for r in range(rep):
            i = hf * rep + r
            cbias = _dot_exact(cb_ref[i], pick_col)
            z = jnp.where(valid_c, _dot_nt(qh_ref[hf, rows(r), :], kc) + cbias, NEG)
            e = jnp.where(valid_c, jnp.exp(z - jnp.max(z, axis=-1, keepdims=True)), 0.0)
            p = e / jnp.maximum(jnp.sum(e, axis=-1, keepdims=True), 1e-30)
            imp = imp + p
            out_ref[i] = gate(0, i) * _dot(p.astype(BF16), vc)
        imp2 = _dot_exact(imp, pair_ref[...])
        qpos = qi * tq + rowl
        cur = qpos // SEL_BLOCK
        forced = (lane == 0) | (lane == cur) | (lane == cur - 1)
        score = jnp.where(lane * SEL_BLOCK > qpos, NEG, jnp.where(forced, SEL_FORCE, imp2))
        st = score.T[:nb8]
        rank = jnp.zeros((nb8, tq), F32)
        for j in range(n_blocks):
            sj = st[j:j + 1, :]
            beats = (sj > st) | ((sj == st) & (blk_t > j))
            rank = rank + jnp.where(beats, 1.0, 0.0)
        sel_t = jnp.where(rank < N_SEL, 1.0, 0.0)
        sel = jnp.concatenate([sel_t, jnp.zeros((LANES - nb8, tq), F32)], axis=0).T.astype(BF16)
        selx_ref[hf] = (_dot(sel, e_ref[...]) - 1.0) * (-NEG)

    n_win_tiles = WINDOW // tq + 1

    def bias_of(i, off, window):
        if off < 2:
            return near_ref[i, off]
        if window and off == n_win_tiles - 1:
            return far_ref[i] + edge_ref[...]
        return far_ref[i]

    def reset():
        mx_ref[...] = jnp.full_like(mx_ref, NEG)
        den_ref[...] = jnp.zeros_like(den_ref)
        acc_ref[...] = jnp.zeros_like(acc_ref)

    def score_tile(hf, kt_ref, start, slot, off, window):
        kt = kt_ref[0, :, pl.ds(start, tq)]
        z = _dot(qh_ref[hf], kt)
        for r in range(rep):
            zr = z[rows(r)] + bias_of(hf * rep + r, off, window)
            if not window:
                zr = zr + selx_ref[hf, :, pl.ds(start, tq)]
            zs_ref[r, :, pl.ds(slot, tq)] = zr
            mx_ref[r] = jnp.maximum(mx_ref[r], zr)

    def fix_max():
        for r in range(rep):
            mx_ref[r] = jnp.broadcast_to(jnp.max(mx_ref[r], axis=-1, keepdims=True), (tq, tq))

    def value_tile(vt_ref, start, slot):
        vt = vt_ref[0, :, pl.ds(start, tq)]
        ps = []
        for r in range(rep):
            p = jnp.exp(zs_ref[r, :, pl.ds(slot, tq)] - mx_ref[r])
            den_ref[r] += p
            ps.append(p.astype(BF16))
        acc_ref[...] += _dot_nt(jnp.concatenate(ps, axis=0), vt)

    def finish(hf, branch):
        for r in range(rep):
            i = hf * rep + r
            den = jnp.sum(den_ref[r], axis=-1, keepdims=True)
            out_ref[i] += (gate(branch, i) / den) * acc_ref[rows(r), :]

    for hf in range(2):
        def sel_score(j, off):
            start = pl.multiple_of(j * tq, tq)
            score_tile(hf, ks_ref, start, start, off, False)

        reset()
        sel_score(qi, 0)
        pl.when(qi >= 1)(lambda: sel_score(qi - 1, 1))

        def far_step(jj, carry):
            sel_score(qi - 2 - jj, 2)
            return carry

        lax.fori_loop(0, jnp.maximum(qi - 1, 0), far_step, 0)
        fix_max()

        def sel_value(j, carry):
            start = pl.multiple_of(j * tq, tq)
            value_tile(vs_ref, start, start)
            return carry

        lax.fori_loop(0, qi + 1, sel_value, 0)
        finish(hf, 1)

        def win_score(off):
            score_tile(hf, kw_ref, pl.multiple_of((qi - off) * tq, tq), off * tq, off, True)

        def win_value(off):
            value_tile(vw_ref, pl.multiple_of((qi - off) * tq, tq), off * tq)

        reset()
        win_score(0)
        for off in range(1, n_win_tiles):
            pl.when(qi >= off)(functools.partial(win_score, off))
        fix_max()
        win_value(0)
        for off in range(1, n_win_tiles):
            pl.when(qi >= off)(functools.partial(win_value, off))
        finish(hf, 2)

    for c in range(n_heads // 2):
        halves = []
        for i in (2 * c, 2 * c + 1):
            o = out_ref[i]
            if i % 2 != i // rep:
                o = pltpu.roll(o, HEAD_DIM, 1)
            halves.append(o)
        o_ref[:, c * LANES:(c + 1) * LANES] = jnp.where(lo_half, halves[0], halves[1]).astype(BF16)


def _nsa_prompt(q, kvc, selt, wint, gates, rel_bias, batch):
    m, d = q.shape
    t = m // batch
    tq = min(NSA_TQ, t)
    nq = t // tq
    nc = t // CMP_BLOCK
    nb = t // SEL_BLOCK
    rep = NSA_REP
    n_heads = 2 * rep
    assert t % tq == 0 and nb <= LANES and d == 2 * n_heads * HEAD_DIM and KV_W == 4 * LANES
    assert WINDOW % tq == 0 and WINDOW // tq >= 2 and tq % LANES == 0
    assert _bucket_saturates_from(tq + 1, t + WINDOW)
    n_win_tiles = WINDOW // tq + 1
    zs_width = max(t, n_win_tiles * tq)

    tl = np.arange(tq)
    dist0 = tl[:, None] - tl[None, :]
    near = jnp.stack([_bias_of(rel_bias, dist0) + jnp.where(dist0 >= 0, 0.0, NEG),
                      _bias_of(rel_bias, tq + dist0)], axis=1)
    far = _bias_of(rel_bias, np.full((1, tq), 2 * tq))
    edge = jnp.asarray(np.where((n_win_tiles - 1) * tq + dist0 <= WINDOW, 0.0, NEG), F32)
    shift = tq // CMP_BLOCK - 1
    n_near = shift + -(-(tq + CMP_BLOCK) // CMP_BLOCK) + 1
    assert tq % CMP_BLOCK == 0 and CMP_BLOCK * (n_near - shift) - CMP_BLOCK + 1 > tq
    assert n_near < LANES
    cb = _bias_of(rel_bias, CMP_BLOCK * (np.arange(LANES)[None, :] - shift) + tl[:, None] - (CMP_BLOCK - 1))
    expand = jnp.asarray(np.arange(LANES)[:, None] == (np.arange(t) // SEL_BLOCK)[None, :], BF16)
    pair = jnp.asarray((np.arange(nc) // (SEL_BLOCK // CMP_BLOCK))[:, None] == np.arange(LANES)[None, :], BF16)

    kvc_spec = lambda col0: pl.BlockSpec((nc, LANES), lambda b, gp, i: (b, col0 + gp))
    kvt_spec = lambda row0: pl.BlockSpec((1, LANES, t), lambda b, gp, i: (b, row0 + gp, 0))
    return pl.pallas_call(
        functools.partial(_nsa_prompt_body, n_near=n_near), grid=(batch, 2, nq),
        in_specs=[pl.BlockSpec((tq, n_heads * HEAD_DIM), lambda b, gp, i: (b * nq + i, gp)),
                  kvc_spec(0), kvc_spec(2), kvt_spec(0), kvt_spec(2), kvt_spec(0), kvt_spec(2),
                  pl.BlockSpec((tq, LANES), lambda b, gp, i: (b * nq + i, gp)),
                  pl.BlockSpec((n_heads, 2, tq, tq), lambda b, gp, i: (gp, 0, 0, 0)),
                  pl.BlockSpec((n_heads, 1, tq), lambda b, gp, i: (gp, 0, 0)),
                  pl.BlockSpec((tq, tq), lambda b, gp, i: (0, 0)),
                  pl.BlockSpec((n_heads, tq, LANES), lambda b, gp, i: (gp, 0, 0)),
                  pl.BlockSpec((LANES, t), lambda b, gp, i: (0, 0)),
                  pl.BlockSpec((nc, LANES), lambda b, gp, i: (0, 0))],
        out_specs=pl.BlockSpec((tq, n_heads * HEAD_DIM), lambda b, gp, i: (b * nq + i, gp)),
        out_shape=jax.ShapeDtypeStruct((m, d), BF16),
        scratch_shapes=[pltpu.VMEM((2, rep * tq, LANES), BF16), pltpu.VMEM((2, tq, t), F32),
                        pltpu.VMEM((rep, tq, zs_width), F32), pltpu.VMEM((rep, tq, tq), F32),
                        pltpu.VMEM((rep, tq, tq), F32), pltpu.VMEM((rep * tq, LANES), F32),
                        pltpu.VMEM((n_heads, tq, LANES), F32)],
        compiler_params=_params("parallel", "parallel", "arbitrary"), name="nsa_prompt")(
            q, kvc, kvc, selt, selt, wint, wint, gates, near, far, edge, cb, expand, pair)


def _nsa_select_body(q_ref, kvc_ref, cb_ref, pair_ref, oc_ref, idx_ref, *, past, n_take):
    n_seq = q_ref.shape[0]
    n_cmp = kvc_ref.shape[0] // n_seq
    imps = []
    for s in range(n_seq):
        kvc = kvc_ref[s * n_cmp:(s + 1) * n_cmp, :]
        z = _dot_nt(q_ref[s], kvc[:, :HALF_W]) + cb_ref[...]
        e = jnp.exp(z - jnp.max(z, axis=-1, keepdims=True))
        p = e / jnp.maximum(jnp.sum(e, axis=-1, keepdims=True), 1e-30)
        oc = _dot(p.astype(BF16), kvc[:, HALF_W:])
        for g in range(NSA_GROUPS):
            oc_ref[s, g * NSA_REP:(g + 1) * NSA_REP, :] = oc[g * NSA_REP:(g + 1) * NSA_REP,
                                                             g * HEAD_DIM:(g + 1) * HEAD_DIM]
        imps += [jnp.sum(p[g * NSA_REP:(g + 1) * NSA_REP], axis=0, keepdims=True) for g in range(NSA_GROUPS)]
    imp2 = _dot_exact(jnp.concatenate(imps, axis=0), pair_ref[...])
    rows, width = imp2.shape
    lane = lax.broadcasted_iota(jnp.int32, (rows, width), 1)
    lanef = lane.astype(F32)
    cur = past // SEL_BLOCK
    forced = (lane == 0) | (lane == cur) | (lane == cur - 1)
    score = jnp.where(lane * SEL_BLOCK > past, NEG, jnp.where(forced, SEL_FORCE, imp2))
    out_lane = lax.broadcasted_iota(jnp.int32, (rows, LANES), 1)
    idx = jnp.zeros((rows, LANES), F32)
    for k in range(n_take):
        best = jnp.max(score, axis=-1, keepdims=True)
        pick = jnp.min(jnp.where(score == best, lanef, float(width)), axis=-1, keepdims=True)
        idx = jnp.where(out_lane == k, pick, idx)
        score = jnp.where(lanef == pick, 2.0 * NEG, score)
    idx = idx.astype(jnp.int32)
    for s in range(n_seq):
        idx_ref[s] = idx[s * NSA_GROUPS:(s + 1) * NSA_GROUPS]


def _nsa_select(q16, kvc, cb, past):
    bs, n_heads, _ = q16.shape
    n_cmp = kvc.shape[0] // bs
    n_blocks = past // SEL_BLOCK + 1
    width = -(-n_blocks // LANES) * LANES
    n_take = min(N_SEL, n_blocks)
    n_seq = math.gcd(SELECT_SEQS, bs)
    pair = jnp.asarray((np.arange(n_cmp) // (SEL_BLOCK // CMP_BLOCK))[:, None] == np.arange(width)[None, :], BF16)
    body = functools.partial(_nsa_select_body, past=past, n_take=n_take)
    return pl.pallas_call(
        body, grid=(bs // n_seq,),
        in_specs=[pl.BlockSpec((n_seq, n_heads, HALF_W), lambda b: (b, 0, 0)),
                  pl.BlockSpec((n_seq * n_cmp, KV_W), lambda b: (b, 0)),
                  pl.BlockSpec((n_heads, n_cmp), lambda b: (0, 0)),
                  pl.BlockSpec((n_cmp, width), lambda b: (0, 0))],
        out_specs=[pl.BlockSpec((n_seq, n_heads, HEAD_DIM), lambda b: (b, 0, 0)),
                   pl.BlockSpec((n_seq, NSA_GROUPS, LANES), lambda b: (b, 0, 0))],
        out_shape=[jax.ShapeDtypeStruct((bs, n_heads, HEAD_DIM), F32),
                   jax.ShapeDtypeStruct((bs, NSA_GROUPS, LANES), jnp.int32)],
        compiler_params=_params("parallel"), name="nsa_select")(q16, kvc, cb, pair), n_take


BIAS_TABLE = 2 * LANES


def _lookup_bias(tab, dist):
    n = dist.shape[1]
    onehot = (lax.broadcasted_iota(jnp.int32, (BIAS_TABLE, n), 0) == jnp.minimum(dist, BIAS_TABLE - 1))
    return _dot_exact(tab, jnp.where(onehot, 1.0, 0.0).astype(BF16))


def _nsa_attend_body(pt_ref, ix_ref, q_ref, seln_ref, winn_ref, wcol_ref, win_ref, oc_ref, g_ref, tab_ref, wtab_ref,
                     pick_ref, *rest, past, n_take):
    pages = rest[:n_take]
    o_ref, wout_ref = rest[n_take:]
    b = pl.program_id(0)
    g = pl.program_id(1)
    n_past_blocks = past // SEL_BLOCK
    page = pages[0].shape[2]
    q4 = q_ref[0, 0]
    q4f = q4.astype(F32)
    tab = tab_ref[0]
    gates = g_ref[0, 0]

    has_new = False
    lane = lax.broadcasted_iota(jnp.int32, (1, page), 1)
    kpos, in_block = [], []
    for i in range(n_take):
        v = ix_ref[(b * NSA_GROUPS + g) * n_take + i]
        kp = (v * SEL_BLOCK) // page * page + lane
        kpos.append(kp)
        in_block.append(jnp.where((kp // SEL_BLOCK == v) & (v < n_past_blocks), 1, 0))
        has_new = jnp.logical_or(has_new, v >= n_past_blocks)
    kpos = jnp.concatenate(kpos, axis=1)
    dist = past - kpos
    ok = (jnp.concatenate(in_block, axis=1) == 1) & (dist >= 0)
    kv = jnp.concatenate([pages[i][0] for i in range(n_take)], axis=1).astype(BF16)
    z = jnp.where(ok, _dot(q4, kv) + _lookup_bias(tab, jnp.maximum(dist, 0)), NEG)
    new_s = _bf16_round(seln_ref[0])
    z_new = jnp.sum(q4f * new_s, axis=-1, keepdims=True) + tab[:, :1]
    z_new = jnp.where(has_new, z_new, NEG)
    m = jnp.maximum(jnp.max(z, axis=-1, keepdims=True), z_new)
    e = jnp.exp(z - m)
    e_new = jnp.exp(z_new - m)
    den = jnp.maximum(jnp.sum(e, axis=-1, keepdims=True) + e_new, 1e-30)
    o_s = _dot_nt((e / den).astype(BF16), kv) + _bf16_round(e_new / den) * new_s

    w = win_ref[0]
    wb = w.astype(BF16)
    new_w = _bf16_round(winn_ref[0])
    n_win = w.shape[1]
    zw = _dot(q4, wb) + wtab_ref[0][:, :n_win]
    zw_new = jnp.sum(q4f * new_w, axis=-1, keepdims=True) + wtab_ref[0][:, n_win:n_win + 1]
    mw = jnp.maximum(jnp.max(zw, axis=-1, keepdims=True), zw_new)
    ew = jnp.exp(zw - mw)
    ew_new = jnp.exp(zw_new - mw)
    denw = jnp.maximum(jnp.sum(ew, axis=-1, keepdims=True) + ew_new, 1e-30)
    o_w = _dot_nt((ew / denw).astype(BF16), wb) + _bf16_round(ew_new / denw) * new_w

    o_sw = gates[:, 1:2] * o_s + gates[:, 2:3] * o_w
    o_ref[0, 0] = gates[:, 0:1] * oc_ref[0, 0] + _dot_exact(o_sw, pick_ref[0])

    @pl.when(g == 0)
    def _():
        cols = lax.broadcasted_iota(jnp.int32, w.shape, 1)
        wout_ref[0] = jnp.where(cols == n_win - 1, wcol_ref[0], pltpu.roll(w, n_win - 1, 1))


def _nsa_attend(q4, sel_new, win_new, state_win, layer, cache_sel, page_table, idx, n_take, oc, gates, rel_bias):
    bs = q4.shape[0]
    n_pool, page = cache_sel.shape[1:3]
    n_pages = page_table.shape[1]
    past = n_pages * page
    n_past_blocks = past // SEL_BLOCK
    n_win = state_win.shape[2]
    assert _bucket_saturates_from(BIAS_TABLE - 1, past + SEL_BLOCK) and page % SEL_BLOCK == 0
    pages = _position_minor_pages(cache_sel)
    base = layer * n_pool
    win_t = jnp.transpose(state_win, (0, 1, 3, 4, 5, 2)).reshape(state_win.shape[0] * bs, KV_W, n_win)
    tab = _bias_of(rel_bias, np.arange(BIAS_TABLE)).reshape(NSA_GROUPS, NSA_REP, BIAS_TABLE)
    wpad = -(-(n_win + 1) // LANES) * LANES
    wdist = np.maximum(n_win - np.arange(wpad), 0)
    wtab = _bias_of(rel_bias, wdist).reshape(NSA_GROUPS, NSA_REP, wpad)
    feats = np.arange(KV_W)[None, :, None]
    pick = jnp.asarray(feats == HALF_W + np.arange(NSA_GROUPS)[:, None, None] * HEAD_DIM
                       + np.arange(HEAD_DIM)[None, None, :], BF16)

    def page_spec(i):
        def index(b, g, pt, ix):
            v = jnp.minimum(ix[(b * NSA_GROUPS + g) * n_take + i], n_past_blocks - 1)
            return (base + pt[b, (v * SEL_BLOCK) // page], 0, 0)
        return pl.BlockSpec((1, KV_W, page), index)

    grid_spec = pltpu.PrefetchScalarGridSpec(
        num_scalar_prefetch=2, grid=(bs, NSA_GROUPS),
        in_specs=[pl.BlockSpec((1, 1, NSA_REP, KV_W), lambda b, g, pt, ix: (b, g, 0, 0)),
                  pl.BlockSpec((1, 1, KV_W), lambda b, g, pt, ix: (b, 0, 0)),
                  pl.BlockSpec((1, 1, KV_W), lambda b, g, pt, ix: (b, 0, 0)),
                  pl.BlockSpec((1, KV_W, 1), lambda b, g, pt, ix: (b, 0, 0)),
                  pl.BlockSpec((1, KV_W, n_win), lambda b, g, pt, ix: (layer * bs + b, 0, 0)),
                  pl.BlockSpec((1, 1, NSA_REP, HEAD_DIM), lambda b, g, pt, ix: (b, g, 0, 0)),
                  pl.BlockSpec((1, 1, NSA_REP, LANES), lambda b, g, pt, ix: (b, g, 0, 0)),
                  pl.BlockSpec((1, NSA_REP, BIAS_TABLE), lambda b, g, pt, ix: (g, 0, 0)),
                  pl.BlockSpec((1, NSA_REP, wpad), lambda b, g, pt, ix: (g, 0, 0)),
                  pl.BlockSpec((1, KV_W, HEAD_DIM), lambda b, g, pt, ix: (g, 0, 0))]
                 + [page_spec(i) for i in range(n_take)],
        out_specs=[pl.BlockSpec((1, 1, NSA_REP, HEAD_DIM), lambda b, g, pt, ix: (b, g, 0, 0)),
                   pl.BlockSpec((1, KV_W, n_win), lambda b, g, pt, ix: (b, 0, 0))])
    body = functools.partial(_nsa_attend_body, past=past, n_take=n_take)
    return pl.pallas_call(
        body, grid_spec=grid_spec,
        out_shape=[jax.ShapeDtypeStruct((bs, NSA_GROUPS, NSA_REP, HEAD_DIM), F32),
                   jax.ShapeDtypeStruct((bs, KV_W, n_win), F32)],
        compiler_params=_params("parallel", "arbitrary"), name="nsa_attend")(
            page_table, idx, q4, sel_new, win_new, win_new.reshape(bs, KV_W, 1), win_t, oc, gates, tab, wtab, pick,
            *([pages] * n_take))


def _compress_weights(cmp_pos, w_cmp):
    pos = jnp.broadcast_to(jnp.swapaxes(cmp_pos, 0, 1)[:, :, None, :], (CMP_BLOCK, 2, NSA_GROUPS, HEAD_DIM))
    eye = jnp.eye(NSA_GROUPS, dtype=w_cmp.dtype)
    w2 = jnp.einsum('clde,gh->lcgdhe', w_cmp, eye).reshape(CMP_BLOCK, 2, HALF_W, HALF_W)
    return pos.reshape(CMP_BLOCK, KV_W), w2.astype(BF16)


def _nsa_gate_weights(wg, paired):
    d = wg.shape[0]
    if paired:
        w = wg.reshape(d, 3, 2, 2 * NSA_REP).transpose(0, 2, 1, 3).reshape(d, 2, 3 * 2 * NSA_REP)
        w = jnp.pad(w, ((0, 0), (0, 0), (0, LANES - w.shape[2])))
        return w.reshape(d, 2 * LANES).astype(BF16)
    return jnp.pad(wg, ((0, 0), (0, LANES - wg.shape[1]))).astype(BF16)


def _rows_by_position(x_t, lead):
    batch, _, t = x_t.shape
    nd = len(lead)
    return jnp.transpose(x_t.reshape(batch, *lead, t), (0, nd + 1) + tuple(range(1, nd + 1)))


def kernel(x_prompt, x_sample, cache_sb_kv, cache_cmp_kv, cache_sel_kv, state_win_kv, page_table,
           w_in_sb, w_out_sb, w_in_nsa, w_out_nsa, cmp_pos, w_cmp, rel_bias, ln_g, ln_b, w_up, w_down):
    bp, t, d = x_prompt.shape
    bs, tn, _ = x_sample.shape
    assert tn == 1 and d == NSA_GROUPS * NSA_REP * HEAD_DIM
    depth = ln_g.shape[0]
    alpha = (2 * depth) ** 0.25
    past = page_table.shape[1] * cache_sb_kv.shape[2]
    n_heads = d // HEAD_DIM
    sb_lead = (2, n_heads, HEAD_DIM)
    nsa_lead = (2, NSA_GROUPS, HEAD_DIM)
    xp = x_prompt.reshape(bp * t, d)
    xs = x_sample.reshape(bs, d)
    sb_p, sb_s, cmp_p, cmp_s, sel_p, sel_s, win_p, win_s = [], [], [], [], [], [], [], []
    for i in range(depth):
        l = i // 2
        if i % 2 == 0:
            w = w_in_sb[l].astype(BF16)
            wq, wkv = w[:, :d], w[:, d:]
            q, v, kvt, kt = _sb_proj(xp, wq, w[:, 2 * d:], wkv.T, bp)
            mp = _sb_prompt(q, kt, v, bp)
            qs, kvs = _sb_proj_rows(xs, wq, wkv)
            ms = _sb_sample(qs, cache_sb_kv, l, page_table)
            sb_p.append(_rows_by_position(kvt, sb_lead))
            sb_s.append(kvs.reshape(bs, tn, *sb_lead))
            w_out = w_out_sb[l]
        else:
            w = w_in_nsa[l].astype(BF16)
            wq = w[:, :d]
            wkv = w[:, d:d + 3 * KV_W]
            wg = w_in_nsa[l][:, d + 3 * KV_W:]
            pos_rows, w2 = _compress_weights(cmp_pos[l], w_cmp[l])
            q, c_rows, gates, c_t, s_t, w_t, s_tb, w_tb = _nsa_proj(
                xp, wq, wkv[:, :KV_W], _nsa_gate_weights(wg, True), wkv.T, bp)
            kvc = _compress(c_rows.reshape(bp * t // CMP_BLOCK, CMP_BLOCK * KV_W), pos_rows.reshape(1, -1), w2)
            mp = _nsa_prompt(q, kvc, s_tb, w_tb, gates, rel_bias, bp)
            w_buf = min(WINDOW, t)
            cmp_p.append(_rows_by_position(c_t, nsa_lead))
            sel_p.append(_rows_by_position(s_t, nsa_lead))
            win_p.append(_rows_by_position(w_t[:, :, t - w_buf:], nsa_lead))
            qs, kv_s, gates_s = _nsa_proj_rows(xs, wq, wkv, _nsa_gate_weights(wg, False))
            c_s, s_s, w_s = kv_s[:, :KV_W], kv_s[:, KV_W:2 * KV_W], kv_s[:, 2 * KV_W:]
            kvc_s = _compress_paged(cache_cmp_kv, l, page_table, pos_rows, w2)
            qh = qs.reshape(bs, NSA_GROUPS, NSA_REP, 1, HEAD_DIM)
            eye = jnp.eye(NSA_GROUPS, dtype=qs.dtype)[None, :, None, :, None]
            q16 = (qh * eye).reshape(bs, n_heads, HALF_W)
            q4 = jnp.pad(q16, ((0, 0), (0, 0), (0, HALF_W))).reshape(bs, NSA_GROUPS, NSA_REP, KV_W)
            n_cmp = past // CMP_BLOCK
            cb_s = _bias_of(rel_bias, past - (np.arange(n_cmp) * CMP_BLOCK + CMP_BLOCK - 1))
            (oc, idx), n_take = _nsa_select(q16, kvc_s, cb_s, past)
            g3 = gates_s[:, :3 * n_heads].reshape(bs, 3, NSA_GROUPS, NSA_REP).transpose(0, 2, 3, 1)
            g3 = jnp.pad(g3, ((0, 0), (0, 0), (0, 0), (0, LANES - 3)))
            o4, wnew = _nsa_attend(q4, s_s.reshape(bs, 1, KV_W), w_s.reshape(bs, 1, KV_W), state_win_kv, l,
                                   cache_sel_kv, page_table, idx[:, :, :n_take].reshape(-1), n_take,
                                   oc.reshape(bs, NSA_GROUPS, NSA_REP, HEAD_DIM), g3, rel_bias)
            ms = o4.reshape(bs, d)
            cmp_s.append(c_s.reshape(bs, tn, *nsa_lead))
            sel_s.append(s_s.reshape(bs, tn, *nsa_lead))
            win_s.append(_rows_by_position(wnew, nsa_lead))
            w_out = w_out_nsa[l]
        tail_args = (w_out.astype(BF16), ln_g[i, 0][None], ln_b[i, 0][None], w_up[i].astype(BF16),
                     w_down[i].astype(BF16), ln_g[i, 1][None], ln_b[i, 1][None], alpha)
        xp = _tail(xp, mp, *tail_args)
        xs = _tail(xs, ms, *tail_args)
    return (xp.reshape(bp, t, d), xs.reshape(bs, tn, d), jnp.stack(sb_p), jnp.stack(sb_s), jnp.stack(cmp_p),
            jnp.stack(cmp_s), jnp.stack(sel_p), jnp.stack(sel_s), jnp.stack(win_p), jnp.stack(win_s))
```

```python
import functools
import math

import numpy as np
import jax
import jax.numpy as jnp
from jax import lax
from jax.experimental import pallas as pl
from jax.experimental.pallas import tpu as pltpu

F32 = jnp.float32
BF16 = jnp.bfloat16

HEAD_DIM = 64
NSA_GROUPS = 4
NSA_REP = 4
CMP_BLOCK = 32
SEL_BLOCK = 64
N_SEL = 16
WINDOW = 512
N_BUCKETS = 32
MAX_DISTANCE = 128
LN_EPS = 1e-5
SEL_FORCE = 1e4
NEG = -1e30

LANES = 128
ROW_TILE = 256
TAIL_TILE = 512
SB_TILE = 256
NSA_TQ = 256
SB_PAGES_PER_STEP = 16
CMP_PAGES_PER_STEP = 16
CMP_DOT_ROWS = 256
SELECT_SEQS = 8
VMEM_LIMIT = 56 * 1024 * 1024

KV_W = 2 * NSA_GROUPS * HEAD_DIM
HALF_W = NSA_GROUPS * HEAD_DIM

_NT = (((1,), (1,)), ((), ()))


def _dot(a, b):
    return jnp.dot(a, b, preferred_element_type=F32)


def _dot_nt(a, b):
    return lax.dot_general(a, b, _NT, preferred_element_type=F32)


def _split2(x):
    hi = x.astype(BF16)
    lo = (x - hi.astype(F32)).astype(BF16)
    return hi, lo


def _split3(x):
    hi = x.astype(BF16)
    r = x - hi.astype(F32)
    mid = r.astype(BF16)
    lo = (r - mid.astype(F32)).astype(BF16)
    return hi, mid, lo


def _dot_exact(x, m):
    hi, mid, lo = _split3(x)
    return _dot(hi, m) + _dot(mid, m) + _dot(lo, m)


def _softplus(z):
    return jnp.maximum(z, 0.0) + jnp.log(1.0 + jnp.exp2(jnp.abs(z) * (-1.0 / math.log(2.0))))


def _layer_norm(y, g, b):
    mu = jnp.mean(y, axis=-1, keepdims=True)
    yc = y - mu
    var = jnp.mean(yc * yc, axis=-1, keepdims=True)
    return yc * lax.rsqrt(var + LN_EPS) * g + b


def _bf16_round(x):
    return x.astype(BF16).astype(F32)


def _params(*sem):
    return pltpu.CompilerParams(dimension_semantics=sem, vmem_limit_bytes=VMEM_LIMIT)


def _const_spec(a):
    nd = a.ndim
    return pl.BlockSpec(a.shape, lambda *_: (0,) * nd, pipeline_mode=pl.Buffered(1))


def _row_call(body, row_args, const_args, outs, name, t_outs=(), seq_len=None, row_tile=ROW_TILE):
    m = row_args[0].shape[0]
    tm = min(row_tile, m)
    assert m % tm == 0
    in_specs = [pl.BlockSpec((tm, a.shape[1]), lambda i: (i, 0)) for a in row_args]
    in_specs += [_const_spec(a) for a in const_args]
    out_specs = [pl.BlockSpec((tm, o.shape[1]), lambda i: (i, 0)) for o in outs]
    if t_outs:
        nt = seq_len // tm
        assert seq_len % tm == 0
        out_specs += [pl.BlockSpec((1, o.shape[1], tm), lambda i: (i // nt, 0, i % nt)) for o in t_outs]
    return pl.pallas_call(
        body, grid=(m // tm,), in_specs=in_specs, out_specs=out_specs, out_shape=list(outs) + list(t_outs),
        compiler_params=_params("parallel"), name=name)(*row_args, *const_args)


def _sb_proj_rows_body(x_ref, wq_ref, wkv_ref, q_ref, kv_ref):
    xb = x_ref[...].astype(BF16)
    q_ref[...] = (_dot(xb, wq_ref[...]) * HEAD_DIM ** -0.5).astype(BF16)
    kv_ref[...] = _dot(xb, wkv_ref[...])


def _sb_proj_rows(x, wq, wkv):
    m, d = x.shape
    outs = [jax.ShapeDtypeStruct((m, d), BF16), jax.ShapeDtypeStruct((m, 2 * d), F32)]
    return _row_call(_sb_proj_rows_body, [x], [wq, wkv], outs, "sb_proj_rows")


def _sb_proj_body(x_ref, wq_ref, wkvt_ref, q_ref, kvt_ref, kvtb_ref):
    xb = x_ref[...].astype(BF16)
    q_ref[...] = (_dot(xb, wq_ref[...]) * HEAD_DIM ** -0.5).astype(BF16)
    kvt = _dot_nt(wkvt_ref[...], xb)
    kvt_ref[0] = kvt
    kvtb_ref[0] = kvt.astype(BF16)


def _sb_proj(x, wq, wkvt, batch):
    m, d = x.shape
    t = m // batch
    outs = [jax.ShapeDtypeStruct((m, d), BF16)]
    t_outs = [jax.ShapeDtypeStruct((batch, 2 * d, t), F32), jax.ShapeDtypeStruct((batch, 2 * d, t), BF16)]
    return _row_call(_sb_proj_body, [x], [wq, wkvt], outs, "sb_proj", t_outs, t)


def _tail_body(x_ref, m_ref, wo_ref, g1_ref, b1_ref, wu_ref, wd_ref, g2_ref, b2_ref, o_ref, *, alpha):
    y = alpha * x_ref[...] + _dot(m_ref[...].astype(BF16), wo_ref[...])
    y = _layer_norm(y, g1_ref[...], b1_ref[...])
    h = jnp.maximum(_dot(y.astype(BF16), wu_ref[...]), 0.0)
    y = alpha * y + _dot((h * h).astype(BF16), wd_ref[...])
    o_ref[...] = _layer_norm(y, g2_ref[...], b2_ref[...])


def _tail(x, mix, wo, g1, b1, wu, wd, g2, b2, alpha):
    outs = [jax.ShapeDtypeStruct(x.shape, F32)]
    body = functools.partial(_tail_body, alpha=alpha)
    return _row_call(body, [x, mix], [wo, g1, b1, wu, wd, g2, b2], outs, "block_tail", row_tile=TAIL_TILE)[0]


def _rev_cumsum_matrix(n):
    j = np.arange(n)
    u = j[:, None] >= j[None, :]
    return jnp.asarray(np.concatenate([u, u], axis=0), BF16)


def _sb_local(z, u, mask):
    sp = _softplus(z)
    if mask is not None:
        sp = jnp.where(mask, sp, 0.0)
    cum = _dot(jnp.concatenate(_split2(sp), axis=1), u)
    return z - cum, cum[:, :1]


def _sb_weights(d, suf, mask):
    a = jnp.exp(d - suf)
    if mask is not None:
        a = jnp.where(mask, a, 0.0)
    return a.astype(BF16)


def _sb_prompt_body(q_ref, kt_ref, vt_ref, u_ref, o_ref, acc_ref):
    tq = q_ref.shape[0]
    qi = pl.program_id(2)
    lane = lax.broadcasted_iota(jnp.int32, (tq, LANES), 1)
    lo_half = lane < HEAD_DIM
    q2 = q_ref[...]
    zero = jnp.zeros_like(q2)
    qs = (jnp.where(lo_half, q2, zero), jnp.where(lo_half, zero, q2))
    u = u_ref[...]
    row = lax.broadcasted_iota(jnp.int32, (tq, tq), 0)
    col = lax.broadcasted_iota(jnp.int32, (tq, tq), 1)
    strictly_before = col < row
    acc_ref[...] = jnp.zeros_like(acc_ref)

    def tiles(js, sufs, mask):
        local = []
        for j in js:
            start = pl.multiple_of(j * tq, tq)
            kt2 = kt_ref[0, :, pl.ds(start, tq)]
            vt2 = vt_ref[0, :, pl.ds(start, tq)]
            local.append((vt2, [_sb_local(_dot(qs[h], kt2), u, mask) for h in range(2)]))
        sufs = list(sufs)
        for vt2, per_head in local:
            for h, (d, tot) in enumerate(per_head):
                acc_ref[h] += _dot_nt(_sb_weights(d, sufs[h], mask), vt2)
                sufs[h] = sufs[h] + tot
        return tuple(sufs)

    zero_suf = jnp.zeros((tq, 1), F32)
    sufs = tiles([qi], (zero_suf, zero_suf), strictly_before)
    sufs = lax.fori_loop(0, qi // 2, lambda jj, s: tiles([qi - 1 - 2 * jj, qi - 2 - 2 * jj], s, None), sufs)

    @pl.when(qi % 2 == 1)
    def _():
        tiles([0], sufs, None)

    o_ref[...] = jnp.where(lo_half, acc_ref[0], acc_ref[1]).astype(BF16)


def _sb_prompt(q, kvt, batch):
    m, d = q.shape
    n_hp = d // LANES
    t = m // batch
    tq = min(SB_TILE, t)
    assert t % tq == 0 and d % LANES == 0
    nq = t // tq
    u = _rev_cumsum_matrix(tq)
    return pl.pallas_call(
        _sb_prompt_body, grid=(batch, d // LANES, nq),
        in_specs=[pl.BlockSpec((tq, LANES), lambda b, h, i: (b * nq + i, h)),
                  pl.BlockSpec((1, LANES, t), lambda b, h, i: (b, h, 0)),
                  pl.BlockSpec((1, LANES, t), lambda b, h, i: (b, n_hp + h, 0)),
                  pl.BlockSpec((2 * tq, tq), lambda b, h, i: (0, 0))],
        out_specs=pl.BlockSpec((tq, LANES), lambda b, h, i: (b * nq + i, h)),
        out_shape=jax.ShapeDtypeStruct((m, d), BF16),
        scratch_shapes=[pltpu.VMEM((2, tq, LANES), F32)],
        compiler_params=_params("parallel", "parallel", "arbitrary"), name="sb_prompt")(q, kvt, kvt, u)


def _position_minor_pages(cache):
    n_layers, n_pool, page = cache.shape[:3]
    feat = cache.shape[3] * cache.shape[4] * cache.shape[5]
    return jnp.transpose(cache, (0, 1, 3, 4, 5, 2)).reshape(n_layers * n_pool, feat, page)


def _sb_sample_body(pt_ref, q_ref, u_ref, *rest, pages_per_step):
    pages = rest[:pages_per_step]
    o_ref, acc_ref, suf_ref = rest[pages_per_step:]
    s = pl.program_id(1)
    d = q_ref.shape[-1]
    heads = d // HEAD_DIM

    @pl.when(s == 0)
    def _():
        acc_ref[...] = jnp.zeros_like(acc_ref)
        suf_ref[...] = jnp.zeros_like(suf_ref)

    own = (lax.broadcasted_iota(jnp.int32, (heads, d), 1) // HEAD_DIM
           == lax.broadcasted_iota(jnp.int32, (heads, d), 0))
    qrow = jnp.broadcast_to(q_ref[0].astype(F32), (heads, d))
    qbd = jnp.where(own, qrow, 0.0).astype(BF16)
    u = u_ref[...]
    acc = acc_ref[...]
    suf = suf_ref[...]
    local = []
    for p in range(pages_per_step):
        kt = pages[p][0, :d, :].astype(BF16)
        local.append(_sb_local(_dot(qbd, kt), u, None))
    for p, (dl, tot) in enumerate(local):
        vt = pages[p][0, d:, :].astype(BF16)
        acc = acc + _dot_nt(_sb_weights(dl, suf, None), vt)
        suf = suf + tot
    acc_ref[...] = acc
    suf_ref[...] = suf

    @pl.when(s == pl.num_programs(1) - 1)
    def _():
        o_ref[0] = jnp.sum(jnp.where(own, acc, 0.0), axis=0, keepdims=True)


def _sb_sample(q, cache, layer, page_table):
    b, d = q.shape
    n_pool, page = cache.shape[1:3]
    assert cache.shape[3] * cache.shape[4] * cache.shape[5] == 2 * d
    n_pages = page_table.shape[1]
    pps = math.gcd(SB_PAGES_PER_STEP, n_pages)
    pages = _position_minor_pages(cache)
    base = layer * n_pool

    def page_spec(p):
        return pl.BlockSpec((1, 2 * d, page),
                            lambda i, s, pt: (base + pt[i, n_pages - 1 - (s * pps + p)], 0, 0))

    grid_spec = pltpu.PrefetchScalarGridSpec(
        num_scalar_prefetch=1, grid=(b, n_pages // pps),
        in_specs=[pl.BlockSpec((1, 1, d), lambda i, s, pt: (i, 0, 0)),
                  pl.BlockSpec((2 * page, page), lambda i, s, pt: (0, 0))]
                 + [page_spec(p) for p in range(pps)],
        out_specs=pl.BlockSpec((1, 1, d), lambda i, s, pt: (i, 0, 0)),
        scratch_shapes=[pltpu.VMEM((d // HEAD_DIM, d), F32), pltpu.VMEM((d // HEAD_DIM, 1), F32)])
    out = pl.pallas_call(
        functools.partial(_sb_sample_body, pages_per_step=pps), grid_spec=grid_spec,
        out_shape=jax.ShapeDtypeStruct((b, 1, d), F32),
        compiler_params=_params("parallel", "arbitrary"), name="sb_sample")(
            page_table, q.reshape(b, 1, d), _rev_cumsum_matrix(page), *([pages] * pps))
    return out.reshape(b, d)


def _nsa_proj_rows_body(x_ref, wq_ref, wkv_ref, wg_ref, q_ref, kv_ref, g_ref):
    xb = x_ref[...].astype(BF16)
    q_ref[...] = (_dot(xb, wq_ref[...]) * HEAD_DIM ** -0.5).astype(BF16)
    kv_ref[...] = _dot(xb, wkv_ref[...])
    g_ref[...] = jax.nn.sigmoid(_dot(xb, wg_ref[...]))


def _nsa_proj_rows(x, wq, wkv, wg):
    m, d = x.shape
    outs = [jax.ShapeDtypeStruct((m, d), BF16), jax.ShapeDtypeStruct((m, wkv.shape[1]), F32),
            jax.ShapeDtypeStruct((m, wg.shape[1]), F32)]
    return _row_call(_nsa_proj_rows_body, [x], [wq, wkv, wg], outs, "nsa_proj_rows")


def _nsa_proj_body(x_ref, wq_ref, wc_ref, wg_ref, wkvt_ref, q_ref, cmp_ref, g_ref,
                   cmpt_ref, selt_ref, wint_ref, seltb_ref, wintb_ref):
    xb = x_ref[...].astype(BF16)
    q_ref[...] = (_dot(xb, wq_ref[...]) * HEAD_DIM ** -0.5).astype(BF16)
    cmp_ref[...] = _dot(xb, wc_ref[...])
    g_ref[...] = jax.nn.sigmoid(_dot(xb, wg_ref[...]))
    kvt = _dot_nt(wkvt_ref[...], xb)
    cmpt_ref[0] = kvt[:KV_W]
    sel = kvt[KV_W:2 * KV_W]
    win = kvt[2 * KV_W:]
    selt_ref[0] = sel
    wint_ref[0] = win
    seltb_ref[0] = sel.astype(BF16)
    wintb_ref[0] = win.astype(BF16)


def _nsa_proj(x, wq, wc, wg, wkvt, batch):
    m, d = x.shape
    t = m // batch
    outs = [jax.ShapeDtypeStruct((m, d), BF16), jax.ShapeDtypeStruct((m, KV_W), F32),
            jax.ShapeDtypeStruct((m, wg.shape[1]), F32)]
    t_outs = [jax.ShapeDtypeStruct((batch, KV_W, t), F32)] * 3 + [jax.ShapeDtypeStruct((batch, KV_W, t), BF16)] * 2
    return _row_call(_nsa_proj_body, [x], [wq, wc, wg, wkvt], outs, "nsa_proj", t_outs, t)


CMP_L_CHUNK = 4


def _compress_body(x_ref, pos_ref, w_ref, o_ref, acc_ref):
    kc = pl.program_id(1)

    @pl.when(kc == 0)
    def _():
        acc_ref[...] = jnp.zeros_like(acc_ref)

    xb = (x_ref[...] + pos_ref[...]).astype(BF16)
    for l in range(CMP_L_CHUNK):
        for c in range(2):
            lo = l * KV_W + c * HALF_W
            acc_ref[:, c * HALF_W:(c + 1) * HALF_W] += _dot(xb[:, lo:lo + HALF_W], w_ref[l, c])

    @pl.when(kc == pl.num_programs(1) - 1)
    def _():
        o_ref[...] = acc_ref[...].astype(BF16)


def _compress(x2, pos_flat, w2):
    rows, width = x2.shape
    tm = min(512, rows)
    assert rows % tm == 0 and CMP_BLOCK % CMP_L_CHUNK == 0
    kw = CMP_L_CHUNK * KV_W
    return pl.pallas_call(
        _compress_body, grid=(rows // tm, width // kw),
        in_specs=[pl.BlockSpec((tm, kw), lambda i, k: (i, k)),
                  pl.BlockSpec((1, kw), lambda i, k: (0, k)),
                  pl.BlockSpec((CMP_L_CHUNK, 2, HALF_W, HALF_W), lambda i, k: (k, 0, 0, 0))],
        out_specs=pl.BlockSpec((tm, KV_W), lambda i, k: (i, 0)),
        out_shape=jax.ShapeDtypeStruct((rows, KV_W), BF16),
        scratch_shapes=[pltpu.VMEM((tm, KV_W), F32)],
        compiler_params=_params("parallel", "arbitrary"), name="nsa_compress")(x2, pos_flat, w2)


def _compress_paged_body(pt_ref, pos_ref, perm_ref, w_ref, *rest, pages_per_step, steps_per_dot):
    pages = rest[:pages_per_step]
    o_ref, x_ref = rest[pages_per_step:]
    s = pl.program_id(1)
    slot = s % steps_per_dot
    pos = pos_ref[...]
    group = perm_ref.shape[0] // pages[0].shape[2]
    slab = perm_ref.shape[0] // CMP_BLOCK
    r0 = pl.multiple_of(slot * (pages_per_step // group) * slab, slab)
    for pg in range(pages_per_step // group):
        a = jnp.concatenate([pages[pg * group + k][0] + pos for k in range(group)], axis=1).astype(BF16)
        xp = _dot_nt(perm_ref[...], a)
        for l in range(CMP_BLOCK):
            x_ref[l, pl.ds(r0 + pg * slab, slab), :] = xp[l * slab:(l + 1) * slab, :]

    @pl.when(slot == steps_per_dot - 1)
    def _():
        for c in range(2):
            xc = jnp.concatenate([x_ref[l, :, c * HALF_W:(c + 1) * HALF_W].astype(BF16)
                                  for l in range(CMP_BLOCK)], axis=1)
            o_ref[:, c * HALF_W:(c + 1) * HALF_W] = _dot(xc, w_ref[c]).astype(BF16)


def _compress_paged(cache, layer, page_table, pos_rows, w2):
    n_pool, page = cache.shape[1:3]
    bs, n_pages = page_table.shape
    rpp = page // CMP_BLOCK
    pages = _position_minor_pages(cache)
    base = layer * n_pool
    pps = math.gcd(CMP_PAGES_PER_STEP, n_pages)
    spd = math.gcd(max(1, CMP_DOT_ROWS // (pps * rpp)), n_pages // pps)
    rows = pps * spd * rpp
    n_steps = n_pages // pps
    group = 8 // rpp
    assert 8 % rpp == 0 and pps % group == 0
    pos_t = jnp.tile(pos_rows, (rpp, 1)).T
    l_, k_, n_ = np.meshgrid(np.arange(CMP_BLOCK), np.arange(group), np.arange(rpp), indexing="ij")
    perm = np.zeros((group * page, group * page), np.float32)
    perm[(l_ * group * rpp + k_ * rpp + n_).ravel(), (k_ * page + n_ * CMP_BLOCK + l_).ravel()] = 1.0

    def page_spec(p):
        return pl.BlockSpec((1, KV_W, page), lambda i, s, pt: (base + pt[i, s * pps + p], 0, 0))

    grid_spec = pltpu.PrefetchScalarGridSpec(
        num_scalar_prefetch=1, grid=(bs, n_steps),
        in_specs=[pl.BlockSpec((KV_W, page), lambda i, s, pt: (0, 0)),
                  pl.BlockSpec(perm.shape, lambda i, s, pt: (0, 0)),
                  pl.BlockSpec((2, CMP_BLOCK * HALF_W, HALF_W), lambda i, s, pt: (0, 0, 0))]
                 + [page_spec(p) for p in range(pps)],
        out_specs=pl.BlockSpec((rows, KV_W), lambda i, s, pt: (i * (n_steps // spd) + s // spd, 0)),
        scratch_shapes=[pltpu.VMEM((CMP_BLOCK, rows, KV_W), F32)])
    body = functools.partial(_compress_paged_body, pages_per_step=pps, steps_per_dot=spd)
    return pl.pallas_call(
        body, grid_spec=grid_spec, out_shape=jax.ShapeDtypeStruct((bs * n_pages * rpp, KV_W), BF16),
        compiler_params=_params("parallel", "arbitrary"), name="nsa_compress_paged")(
            page_table, pos_t, jnp.asarray(perm, BF16),
            jnp.swapaxes(w2, 0, 1).reshape(2, CMP_BLOCK * HALF_W, HALF_W), *([pages] * pps))


def _rel_bucket(dist):
    dist = jnp.maximum(dist, 0)
    max_exact = N_BUCKETS // 2
    dd = jnp.maximum(dist, 1).astype(F32)
    large = max_exact + (jnp.log(dd / max_exact) / math.log(MAX_DISTANCE / max_exact)
                         * (N_BUCKETS - max_exact)).astype(jnp.int32)
    large = jnp.minimum(large, N_BUCKETS - 1)
    return jnp.where(dist < max_exact, dist, large)


def _bias_of(rel_bias, dist):
    onehot = jax.nn.one_hot(_rel_bucket(jnp.asarray(dist, jnp.int32)), N_BUCKETS, dtype=F32)
    return jnp.einsum('...k,kh->h...', onehot, rel_bias.astype(F32), precision=lax.Precision.HIGHEST)


def _bucket_saturates_from(d0, d1):
    dd = np.arange(d0, d1 + 1).astype(np.float32)
    max_exact = N_BUCKETS // 2
    large = max_exact + (np.log(dd / np.float32(max_exact)) / np.float32(math.log(MAX_DISTANCE / max_exact))
                         * np.float32(N_BUCKETS - max_exact)).astype(np.int32)
    return bool(np.all(large >= N_BUCKETS - 1)) and d0 >= max_exact


def _nsa_prompt_body(q_ref, kc_ref, vc_ref, ks_ref, vs_ref, kw_ref, vw_ref, g_ref, near_ref, far_ref, edge_ref,
                     cb_ref, e_ref, pair_ref, o_ref, qh_ref, sel_ref, zs_ref, mx_ref, den_ref, acc_ref, out_ref,
                     *, n_near):
    tq = q_ref.shape[0]
    nc = kc_ref.shape[0]
    rep = NSA_REP
    n_heads = 2 * rep
    qi = pl.program_id(2)
    lane = lax.broadcasted_iota(jnp.int32, (tq, LANES), 1)
    rowl = lax.broadcasted_iota(jnp.int32, (tq, LANES), 0)
    lo_half = lane < HEAD_DIM
    half_mask = (lo_half, jnp.logical_not(lo_half))
    gates = g_ref[...]

    def gate(branch, i):
        k = branch * n_heads + i
        return gates[:, k:k + 1]

    def rows(r):
        return slice(r * tq, (r + 1) * tq)

    for i in range(n_heads):
        hf, r = divmod(i, rep)
        qf = q_ref[:, (i // 2) * LANES:(i // 2 + 1) * LANES].astype(F32)
        if i % 2 != hf:
            qf = pltpu.roll(qf, HEAD_DIM, 1)
        qh_ref[hf, rows(r), :] = jnp.where(half_mask[hf], qf, 0.0).astype(BF16)

    coln = lax.broadcasted_iota(jnp.int32, (tq, nc), 1)
    rown = lax.broadcasted_iota(jnp.int32, (tq, nc), 0)
    valid_c = qi * tq + rown - (coln * CMP_BLOCK + CMP_BLOCK - 1) >= 0
    m_rel = jnp.clip((qi + 1) * (tq // CMP_BLOCK) - 1 - coln[:1], 0, n_near)
    pick_col = jnp.where(lax.broadcasted_iota(jnp.int32, (LANES, nc), 0) == m_rel, 1.0, 0.0).astype(BF16)
    kc = kc_ref[...]
    vc = vc_ref[...]
    n_blocks = ks_ref.shape[2] // SEL_BLOCK
    nb8 = -(-n_blocks // 8) * 8
    blk_t = lax.broadcasted_iota(jnp.int32, (nb8, tq), 0)
    for hf in range(2):
        imp = jnp.zeros((tq, nc), F32)
        for r in range(rep):
            i = hf * rep + r
            cbias = _dot_exact(cb_ref[i], pick_col)
            z = jnp.where(valid_c, _dot_nt(qh_ref[hf, rows(r), :], kc) + cbias, NEG)
            e = jnp.where(valid_c, jnp.exp(z - jnp.max(z, axis=-1, keepdims=True)), 0.0)
            p = e / jnp.maximum(jnp.sum(e, axis=-1, keepdims=True), 1e-30)
            imp = imp + p
            out_ref[i] = gate(0, i) * _dot(p.astype(BF16), vc)
        imp2 = _dot_exact(imp, pair_ref[...])
        qpos = qi * tq + rowl
        cur = qpos // SEL_BLOCK
        forced = (lane == 0) | (lane == cur) | (lane == cur - 1)
        score = jnp.where(lane * SEL_BLOCK > qpos, NEG, jnp.where(forced, SEL_FORCE, imp2))
        st = score.T[:nb8]
        rank = jnp.zeros((nb8, tq), F32)
        for j in range(n_blocks):
            sj = st[j:j + 1, :]
            beats = (sj > st) | ((sj == st) & (blk_t > j))
            rank = rank + jnp.where(beats, 1.0, 0.0)
        sel_t = jnp.where(rank < N_SEL, 1.0, 0.0)
        sel_ref[hf] = jnp.concatenate([sel_t, jnp.zeros((LANES - nb8, tq), F32)], axis=0).T.astype(BF16)

    n_win_tiles = WINDOW // tq + 1

    def bias_of(i, off, window):
        if off < 2:
            return near_ref[i, off]
        if window and off == n_win_tiles - 1:
            return far_ref[i] + edge_ref[...]
        return far_ref[i]

    def reset():
        mx_ref[...] = jnp.full_like(mx_ref, NEG)
        den_ref[...] = jnp.zeros_like(den_ref)
        acc_ref[...] = jnp.zeros_like(acc_ref)

    def score_tile(hf, kt_ref, start, slot, off, window):
        kt = kt_ref[0, :, pl.ds(start, tq)]
        z = _dot(qh_ref[hf], kt)
        if not window:
            smask = (_dot(sel_ref[hf], e_ref[:, pl.ds(start, tq)]) - 1.0) * (-NEG)
        for r in range(rep):
            zr = z[rows(r)] + bias_of(hf * rep + r, off, window)
            if not window:
                zr = zr + smask
            zs_ref[r, :, pl.ds(slot, tq)] = zr
            mx_ref[r] = jnp.maximum(mx_ref[r], zr)

    def fix_max():
        for r in range(rep):
            mx_ref[r] = jnp.broadcast_to(jnp.max(mx_ref[r], axis=-1, keepdims=True), (tq, tq))

    def value_tile(vt_ref, start, slot):
        vt = vt_ref[0, :, pl.ds(start, tq)]
        ps = []
        for r in range(rep):
            p = jnp.exp(zs_ref[r, :, pl.ds(slot, tq)] - mx_ref[r])
            den_ref[r] += p
            ps.append(p.astype(BF16))
        acc_ref[...] += _dot_nt(jnp.concatenate(ps, axis=0), vt)

    def finish(hf, branch):
        for r in range(rep):
            i = hf * rep + r
            den = jnp.sum(den_ref[r], axis=-1, keepdims=True)
            out_ref[i] += (gate(branch, i) / den) * acc_ref[rows(r), :]

    for hf in range(2):
        def sel_score(j, off):
            start = pl.multiple_of(j * tq, tq)
            score_tile(hf, ks_ref, start, start, off, False)

        reset()
        sel_score(qi, 0)
        pl.when(qi >= 1)(lambda: sel_score(qi - 1, 1))

        def far_step(jj, carry):
            sel_score(qi - 2 - jj, 2)
            return carry

        lax.fori_loop(0, jnp.maximum(qi - 1, 0), far_step, 0)
        fix_max()

        def sel_value(j, carry):
            start = pl.multiple_of(j * tq, tq)
            value_tile(vs_ref, start, start)
            return carry

        lax.fori_loop(0, qi + 1, sel_value, 0)
        finish(hf, 1)

        def win_score(off):
            score_tile(hf, kw_ref, pl.multiple_of((qi - off) * tq, tq), off * tq, off, True)

        def win_value(off):
            value_tile(vw_ref, pl.multiple_of((qi - off) * tq, tq), off * tq)

        reset()
        win_score(0)
        for off in range(1, n_win_tiles):
            pl.when(qi >= off)(functools.partial(win_score, off))
        fix_max()
        win_value(0)
        for off in range(1, n_win_tiles):
            pl.when(qi >= off)(functools.partial(win_value, off))
        finish(hf, 2)

    for c in range(n_heads // 2):
        halves = []
        for i in (2 * c, 2 * c + 1):
            o = out_ref[i]
            if i % 2 != i // rep:
                o = pltpu.roll(o, HEAD_DIM, 1)
            halves.append(o)
        o_ref[:, c * LANES:(c + 1) * LANES] = jnp.where(lo_half, halves[0], halves[1]).astype(BF16)


def _nsa_prompt(q, kvc, selt, wint, gates, rel_bias, batch):
    m, d = q.shape
    t = m // batch
    tq = min(NSA_TQ, t)
    nq = t // tq
    nc = t // CMP_BLOCK
    nb = t // SEL_BLOCK
    rep = NSA_REP
    n_heads = 2 * rep
    assert t % tq == 0 and nb <= LANES and d == 2 * n_heads * HEAD_DIM and KV_W == 4 * LANES
    assert WINDOW % tq == 0 and WINDOW // tq >= 2 and tq % LANES == 0
    assert _bucket_saturates_from(tq + 1, t + WINDOW)
    n_win_tiles = WINDOW // tq + 1
    zs_width = max(t, n_win_tiles * tq)

    tl = np.arange(tq)
    dist0 = tl[:, None] - tl[None, :]
    near = jnp.stack([_bias_of(rel_bias, dist0) + jnp.where(dist0 >= 0, 0.0, NEG),
                      _bias_of(rel_bias, tq + dist0)], axis=1)
    far = _bias_of(rel_bias, np.full((1, tq), 2 * tq))
    edge = jnp.asarray(np.where((n_win_tiles - 1) * tq + dist0 <= WINDOW, 0.0, NEG), F32)
    shift = tq // CMP_BLOCK - 1
    n_near = shift + -(-(tq + CMP_BLOCK) // CMP_BLOCK) + 1
    assert tq % CMP_BLOCK == 0 and CMP_BLOCK * (n_near - shift) - CMP_BLOCK + 1 > tq
    assert n_near < LANES
    cb = _bias_of(rel_bias, CMP_BLOCK * (np.arange(LANES)[None, :] - shift) + tl[:, None] - (CMP_BLOCK - 1))
    expand = jnp.asarray(np.arange(LANES)[:, None] == (np.arange(t) // SEL_BLOCK)[None, :], BF16)
    pair = jnp.asarray((np.arange(nc) // (SEL_BLOCK // CMP_BLOCK))[:, None] == np.arange(LANES)[None, :], BF16)

    kvc_spec = lambda col0: pl.BlockSpec((nc, LANES), lambda b, gp, i: (b, col0 + gp))
    kvt_spec = lambda row0: pl.BlockSpec((1, LANES, t), lambda b, gp, i: (b, row0 + gp, 0))
    return pl.pallas_call(
        functools.partial(_nsa_prompt_body, n_near=n_near), grid=(batch, 2, nq),
        in_specs=[pl.BlockSpec((tq, n_heads * HEAD_DIM), lambda b, gp, i: (b * nq + i, gp)),
                  kvc_spec(0), kvc_spec(2), kvt_spec(0), kvt_spec(2), kvt_spec(0), kvt_spec(2),
                  pl.BlockSpec((tq, LANES), lambda b, gp, i: (b * nq + i, gp)),
                  pl.BlockSpec((n_heads, 2, tq, tq), lambda b, gp, i: (gp, 0, 0, 0)),
                  pl.BlockSpec((n_heads, 1, tq), lambda b, gp, i: (gp, 0, 0)),
                  pl.BlockSpec((tq, tq), lambda b, gp, i: (0, 0)),
                  pl.BlockSpec((n_heads, tq, LANES), lambda b, gp, i: (gp, 0, 0)),
                  pl.BlockSpec((LANES, t), lambda b, gp, i: (0, 0)),
                  pl.BlockSpec((nc, LANES), lambda b, gp, i: (0, 0))],
        out_specs=pl.BlockSpec((tq, n_heads * HEAD_DIM), lambda b, gp, i: (b * nq + i, gp)),
        out_shape=jax.ShapeDtypeStruct((m, d), BF16),
        scratch_shapes=[pltpu.VMEM((2, rep * tq, LANES), BF16), pltpu.VMEM((2, tq, LANES), BF16),
                        pltpu.VMEM((rep, tq, zs_width), F32), pltpu.VMEM((rep, tq, tq), F32),
                        pltpu.VMEM((rep, tq, tq), F32), pltpu.VMEM((rep * tq, LANES), F32),
                        pltpu.VMEM((n_heads, tq, LANES), F32)],
        compiler_params=_params("parallel", "parallel", "arbitrary"), name="nsa_prompt")(
            q, kvc, kvc, selt, selt, wint, wint, gates, near, far, edge, cb, expand, pair)


def _nsa_select_body(q_ref, kvc_ref, cb_ref, pair_ref, oc_ref, idx_ref, *, past, n_take):
    n_seq = q_ref.shape[0]
    n_cmp = kvc_ref.shape[0] // n_seq
    imps = []
    for s in range(n_seq):
        kvc = kvc_ref[s * n_cmp:(s + 1) * n_cmp, :]
        z = _dot_nt(q_ref[s], kvc[:, :HALF_W]) + cb_ref[...]
        e = jnp.exp(z - jnp.max(z, axis=-1, keepdims=True))
        p = e / jnp.maximum(jnp.sum(e, axis=-1, keepdims=True), 1e-30)
        oc = _dot(p.astype(BF16), kvc[:, HALF_W:])
        for g in range(NSA_GROUPS):
            oc_ref[s, g * NSA_REP:(g + 1) * NSA_REP, :] = oc[g * NSA_REP:(g + 1) * NSA_REP,
                                                             g * HEAD_DIM:(g + 1) * HEAD_DIM]
        imps += [jnp.sum(p[g * NSA_REP:(g + 1) * NSA_REP], axis=0, keepdims=True) for g in range(NSA_GROUPS)]
    imp2 = _dot_exact(jnp.concatenate(imps, axis=0), pair_ref[...])
    rows, width = imp2.shape
    lane = lax.broadcasted_iota(jnp.int32, (rows, width), 1)
    lanef = lane.astype(F32)
    cur = past // SEL_BLOCK
    forced = (lane == 0) | (lane == cur) | (lane == cur - 1)
    score = jnp.where(lane * SEL_BLOCK > past, NEG, jnp.where(forced, SEL_FORCE, imp2))
    out_lane = lax.broadcasted_iota(jnp.int32, (rows, LANES), 1)
    idx = jnp.zeros((rows, LANES), F32)
    for k in range(n_take):
        best = jnp.max(score, axis=-1, keepdims=True)
        pick = jnp.min(jnp.where(score == best, lanef, float(width)), axis=-1, keepdims=True)
        idx = jnp.where(out_lane == k, pick, idx)
        score = jnp.where(lanef == pick, 2.0 * NEG, score)
    idx = idx.astype(jnp.int32)
    for s in range(n_seq):
        idx_ref[s] = idx[s * NSA_GROUPS:(s + 1) * NSA_GROUPS]


def _nsa_select(q16, kvc, cb, past):
    bs, n_heads, _ = q16.shape
    n_cmp = kvc.shape[0] // bs
    n_blocks = past // SEL_BLOCK + 1
    width = -(-n_blocks // LANES) * LANES
    n_take = min(N_SEL, n_blocks)
    n_seq = math.gcd(SELECT_SEQS, bs)
    pair = jnp.asarray((np.arange(n_cmp) // (SEL_BLOCK // CMP_BLOCK))[:, None] == np.arange(width)[None, :], BF16)
    body = functools.partial(_nsa_select_body, past=past, n_take=n_take)
    return pl.pallas_call(
        body, grid=(bs // n_seq,),
        in_specs=[pl.BlockSpec((n_seq, n_heads, HALF_W), lambda b: (b, 0, 0)),
                  pl.BlockSpec((n_seq * n_cmp, KV_W), lambda b: (b, 0)),
                  pl.BlockSpec((n_heads, n_cmp), lambda b: (0, 0)),
                  pl.BlockSpec((n_cmp, width), lambda b: (0, 0))],
        out_specs=[pl.BlockSpec((n_seq, n_heads, HEAD_DIM), lambda b: (b, 0, 0)),
                   pl.BlockSpec((n_seq, NSA_GROUPS, LANES), lambda b: (b, 0, 0))],
        out_shape=[jax.ShapeDtypeStruct((bs, n_heads, HEAD_DIM), F32),
                   jax.ShapeDtypeStruct((bs, NSA_GROUPS, LANES), jnp.int32)],
        compiler_params=_params("parallel"), name="nsa_select")(q16, kvc, cb, pair), n_take


BIAS_TABLE = 2 * LANES


def _lookup_bias(tab, dist):
    n = dist.shape[1]
    onehot = (lax.broadcasted_iota(jnp.int32, (BIAS_TABLE, n), 0) == jnp.minimum(dist, BIAS_TABLE - 1))
    return _dot_exact(tab, jnp.where(onehot, 1.0, 0.0).astype(BF16))


def _nsa_attend_body(pt_ref, ix_ref, q_ref, seln_ref, winn_ref, wcol_ref, win_ref, oc_ref, g_ref, tab_ref, wtab_ref,
                     pick_ref, *rest, past, n_take):
    pages = rest[:n_take]
    o_ref, wout_ref = rest[n_take:]
    b = pl.program_id(0)
    g = pl.program_id(1)
    n_past_blocks = past // SEL_BLOCK
    page = pages[0].shape[2]
    q4 = q_ref[0, 0]
    q4f = q4.astype(F32)
    tab = tab_ref[0]
    gates = g_ref[0, 0]

    has_new = False
    lane = lax.broadcasted_iota(jnp.int32, (1, page), 1)
    kpos, in_block, bias = [], [], []
    far_bias = jnp.broadcast_to(tab[:, BIAS_TABLE - 1:], (NSA_REP, page))
    for i in range(n_take):
        v = ix_ref[(b * NSA_GROUPS + g) * n_take + i]
        page_start = (v * SEL_BLOCK) // page * page
        kp = page_start + lane
        kpos.append(kp)
        in_block.append(jnp.where((kp // SEL_BLOCK == v) & (v < n_past_blocks), 1, 0))
        has_new = jnp.logical_or(has_new, v >= n_past_blocks)
        bias.append(lax.cond(page_start + page + BIAS_TABLE > past,
                             lambda kp=kp: _lookup_bias(tab, jnp.maximum(past - kp, 0)),
                             lambda: far_bias))
    kpos = jnp.concatenate(kpos, axis=1)
    dist = past - kpos
    ok = (jnp.concatenate(in_block, axis=1) == 1) & (dist >= 0)
    kv = jnp.concatenate([pages[i][0] for i in range(n_take)], axis=1).astype(BF16)
    z = jnp.where(ok, _dot(q4, kv) + jnp.concatenate(bias, axis=1), NEG)
    new_s = _bf16_round(seln_ref[0])
    z_new = jnp.sum(q4f * new_s, axis=-1, keepdims=True) + tab[:, :1]
    z_new = jnp.where(has_new, z_new, NEG)
    m = jnp.maximum(jnp.max(z, axis=-1, keepdims=True), z_new)
    e = jnp.exp(z - m)
    e_new = jnp.exp(z_new - m)
    den = jnp.maximum(jnp.sum(e, axis=-1, keepdims=True) + e_new, 1e-30)
    o_s = _dot_nt((e / den).astype(BF16), kv) + _bf16_round(e_new / den) * new_s

    w = win_ref[0]
    wb = w.astype(BF16)
    new_w = _bf16_round(winn_ref[0])
    n_win = w.shape[1]
    zw = _dot(q4, wb) + wtab_ref[0][:, :n_win]
    zw_new = jnp.sum(q4f * new_w, axis=-1, keepdims=True) + wtab_ref[0][:, n_win:n_win + 1]
    mw = jnp.maximum(jnp.max(zw, axis=-1, keepdims=True), zw_new)
    ew = jnp.exp(zw - mw)
    ew_new = jnp.exp(zw_new - mw)
    denw = jnp.maximum(jnp.sum(ew, axis=-1, keepdims=True) + ew_new, 1e-30)
    o_w = _dot_nt((ew / denw).astype(BF16), wb) + _bf16_round(ew_new / denw) * new_w

    o_sw = gates[:, 1:2] * o_s + gates[:, 2:3] * o_w
    o_ref[0, 0] = gates[:, 0:1] * oc_ref[0, 0] + _dot_exact(o_sw, pick_ref[0])

    @pl.when(g == 0)
    def _():
        cols = lax.broadcasted_iota(jnp.int32, w.shape, 1)
        wout_ref[0] = jnp.where(cols == n_win - 1, wcol_ref[0], pltpu.roll(w, n_win - 1, 1))


def _nsa_attend(q4, sel_new, win_new, state_win, layer, cache_sel, page_table, idx, n_take, oc, gates, rel_bias):
    bs = q4.shape[0]
    n_pool, page = cache_sel.shape[1:3]
    n_pages = page_table.shape[1]
    past = n_pages * page
    n_past_blocks = past // SEL_BLOCK
    n_win = state_win.shape[2]
    assert _bucket_saturates_from(BIAS_TABLE - 1, past + SEL_BLOCK) and page % SEL_BLOCK == 0
    pages = _position_minor_pages(cache_sel)
    base = layer * n_pool
    win_t = jnp.transpose(state_win, (0, 1, 3, 4, 5, 2)).reshape(state_win.shape[0] * bs, KV_W, n_win)
    tab = _bias_of(rel_bias, np.arange(BIAS_TABLE)).reshape(NSA_GROUPS, NSA_REP, BIAS_TABLE)
    wpad = -(-(n_win + 1) // LANES) * LANES
    wdist = np.maximum(n_win - np.arange(wpad), 0)
    wtab = _bias_of(rel_bias, wdist).reshape(NSA_GROUPS, NSA_REP, wpad)
    feats = np.arange(KV_W)[None, :, None]
    pick = jnp.asarray(feats == HALF_W + np.arange(NSA_GROUPS)[:, None, None] * HEAD_DIM
                       + np.arange(HEAD_DIM)[None, None, :], BF16)

    def page_spec(i):
        def index(b, g, pt, ix):
            v = jnp.minimum(ix[(b * NSA_GROUPS + g) * n_take + i], n_past_blocks - 1)
            return (base + pt[b, (v * SEL_BLOCK) // page], 0, 0)
        return pl.BlockSpec((1, KV_W, page), index)

    grid_spec = pltpu.PrefetchScalarGridSpec(
        num_scalar_prefetch=2, grid=(bs, NSA_GROUPS),
        in_specs=[pl.BlockSpec((1, 1, NSA_REP, KV_W), lambda b, g, pt, ix: (b, g, 0, 0)),
                  pl.BlockSpec((1, 1, KV_W), lambda b, g, pt, ix: (b, 0, 0)),
                  pl.BlockSpec((1, 1, KV_W), lambda b, g, pt, ix: (b, 0, 0)),
                  pl.BlockSpec((1, KV_W, 1), lambda b, g, pt, ix: (b, 0, 0)),
                  pl.BlockSpec((1, KV_W, n_win), lambda b, g, pt, ix: (layer * bs + b, 0, 0)),
                  pl.BlockSpec((1, 1, NSA_REP, HEAD_DIM), lambda b, g, pt, ix: (b, g, 0, 0)),
                  pl.BlockSpec((1, 1, NSA_REP, LANES), lambda b, g, pt, ix: (b, g, 0, 0)),
                  pl.BlockSpec((1, NSA_REP, BIAS_TABLE), lambda b, g, pt, ix: (g, 0, 0)),
                  pl.BlockSpec((1, NSA_REP, wpad), lambda b, g, pt, ix: (g, 0, 0)),
                  pl.BlockSpec((1, KV_W, HEAD_DIM), lambda b, g, pt, ix: (g, 0, 0))]
                 + [page_spec(i) for i in range(n_take)],
        out_specs=[pl.BlockSpec((1, 1, NSA_REP, HEAD_DIM), lambda b, g, pt, ix: (b, g, 0, 0)),
                   pl.BlockSpec((1, KV_W, n_win), lambda b, g, pt, ix: (b, 0, 0))])
    body = functools.partial(_nsa_attend_body, past=past, n_take=n_take)
    return pl.pallas_call(
        body, grid_spec=grid_spec,
        out_shape=[jax.ShapeDtypeStruct((bs, NSA_GROUPS, NSA_REP, HEAD_DIM), F32),
                   jax.ShapeDtypeStruct((bs, KV_W, n_win), F32)],
        compiler_params=_params("parallel", "arbitrary"), name="nsa_attend")(
            page_table, idx, q4, sel_new, win_new, win_new.reshape(bs, KV_W, 1), win_t, oc, gates, tab, wtab, pick,
            *([pages] * n_take))


def _compress_weights(cmp_pos, w_cmp):
    pos = jnp.broadcast_to(jnp.swapaxes(cmp_pos, 0, 1)[:, :, None, :], (CMP_BLOCK, 2, NSA_GROUPS, HEAD_DIM))
    eye = jnp.eye(NSA_GROUPS, dtype=w_cmp.dtype)
    w2 = jnp.einsum('clde,gh->lcgdhe', w_cmp, eye).reshape(CMP_BLOCK, 2, HALF_W, HALF_W)
    return pos.reshape(CMP_BLOCK, KV_W), w2.astype(BF16)


def _nsa_gate_weights(wg, paired):
    d = wg.shape[0]
    if paired:
        w = wg.reshape(d, 3, 2, 2 * NSA_REP).transpose(0, 2, 1, 3).reshape(d, 2, 3 * 2 * NSA_REP)
        w = jnp.pad(w, ((0, 0), (0, 0), (0, LANES - w.shape[2])))
        return w.reshape(d, 2 * LANES).astype(BF16)
    return jnp.pad(wg, ((0, 0), (0, LANES - wg.shape[1]))).astype(BF16)


def _rows_by_position(x_t, lead):
    batch, _, t = x_t.shape
    nd = len(lead)
    return jnp.transpose(x_t.reshape(batch, *lead, t), (0, nd + 1) + tuple(range(1, nd + 1)))


def kernel(x_prompt, x_sample, cache_sb_kv, cache_cmp_kv, cache_sel_kv, state_win_kv, page_table,
           w_in_sb, w_out_sb, w_in_nsa, w_out_nsa, cmp_pos, w_cmp, rel_bias, ln_g, ln_b, w_up, w_down):
    bp, t, d = x_prompt.shape
    bs, tn, _ = x_sample.shape
    assert tn == 1 and d == NSA_GROUPS * NSA_REP * HEAD_DIM
    depth = ln_g.shape[0]
    alpha = (2 * depth) ** 0.25
    past = page_table.shape[1] * cache_sb_kv.shape[2]
    n_heads = d // HEAD_DIM
    sb_lead = (2, n_heads, HEAD_DIM)
    nsa_lead = (2, NSA_GROUPS, HEAD_DIM)
    xp = x_prompt.reshape(bp * t, d)
    xs = x_sample.reshape(bs, d)
    sb_p, sb_s, cmp_p, cmp_s, sel_p, sel_s, win_p, win_s = [], [], [], [], [], [], [], []
    for i in range(depth):
        l = i // 2
        if i % 2 == 0:
            w = w_in_sb[l].astype(BF16)
            wq, wkv = w[:, :d], w[:, d:]
            q, kvt, kvt_b = _sb_proj(xp, wq, wkv.T, bp)
            mp = _sb_prompt(q, kvt_b, bp)
            qs, kvs = _sb_proj_rows(xs, wq, wkv)
            ms = _sb_sample(qs, cache_sb_kv, l, page_table)
            sb_p.append(_rows_by_position(kvt, sb_lead))
            sb_s.append(kvs.reshape(bs, tn, *sb_lead))
            w_out = w_out_sb[l]
        else:
            w = w_in_nsa[l].astype(BF16)
            wq = w[:, :d]
            wkv = w[:, d:d + 3 * KV_W]
            wg = w_in_nsa[l][:, d + 3 * KV_W:]
            pos_rows, w2 = _compress_weights(cmp_pos[l], w_cmp[l])
            q, c_rows, gates, c_t, s_t, w_t, s_tb, w_tb = _nsa_proj(
                xp, wq, wkv[:, :KV_W], _nsa_gate_weights(wg, True), wkv.T, bp)
            kvc = _compress(c_rows.reshape(bp * t // CMP_BLOCK, CMP_BLOCK * KV_W), pos_rows.reshape(1, -1), w2)
            mp = _nsa_prompt(q, kvc, s_tb, w_tb, gates, rel_bias, bp)
            w_buf = min(WINDOW, t)
            cmp_p.append(_rows_by_position(c_t, nsa_lead))
            sel_p.append(_rows_by_position(s_t, nsa_lead))
            win_p.append(_rows_by_position(w_t[:, :, t - w_buf:], nsa_lead))
            qs, kv_s, gates_s = _nsa_proj_rows(xs, wq, wkv, _nsa_gate_weights(wg, False))
            c_s, s_s, w_s = kv_s[:, :KV_W], kv_s[:, KV_W:2 * KV_W], kv_s[:, 2 * KV_W:]
            kvc_s = _compress_paged(cache_cmp_kv, l, page_table, pos_rows, w2)
            qh = qs.reshape(bs, NSA_GROUPS, NSA_REP, 1, HEAD_DIM)
            eye = jnp.eye(NSA_GROUPS, dtype=qs.dtype)[None, :, None, :, None]
            q16 = (qh * eye).reshape(bs, n_heads, HALF_W)
            q4 = jnp.pad(q16, ((0, 0), (0, 0), (0, HALF_W))).reshape(bs, NSA_GROUPS, NSA_REP, KV_W)
            n_cmp = past // CMP_BLOCK
            cb_s = _bias_of(rel_bias, past - (np.arange(n_cmp) * CMP_BLOCK + CMP_BLOCK - 1))
            (oc, idx), n_take = _nsa_select(q16, kvc_s, cb_s, past)
            g3 = gates_s[:, :3 * n_heads].reshape(bs, 3, NSA_GROUPS, NSA_REP).transpose(0, 2, 3, 1)
            g3 = jnp.pad(g3, ((0, 0), (0, 0), (0, 0), (0, LANES - 3)))
            o4, wnew = _nsa_attend(q4, s_s.reshape(bs, 1, KV_W), w_s.reshape(bs, 1, KV_W), state_win_kv, l,
                                   cache_sel_kv, page_table, idx[:, :, :n_take].reshape(-1), n_take,
                                   oc.reshape(bs, NSA_GROUPS, NSA_REP, HEAD_DIM), g3, rel_bias)
            ms = o4.reshape(bs, d)
            cmp_s.append(c_s.reshape(bs, tn, *nsa_lead))
            sel_s.append(s_s.reshape(bs, tn, *nsa_lead))
            win_s.append(_rows_by_position(wnew, nsa_lead))
            w_out = w_out_nsa[l]
        tail_args = (w_out.astype(BF16), ln_g[i, 0][None], ln_b[i, 0][None], w_up[i].astype(BF16),
                     w_down[i].astype(BF16), ln_g[i, 1][None], ln_b[i, 1][None], alpha)
        xp = _tail(xp, mp, *tail_args)
        xs = _tail(xs, ms, *tail_args)
    return (xp.reshape(bp, t, d), xs.reshape(bs, tn, d), jnp.stack(sb_p), jnp.stack(sb_s), jnp.stack(cmp_p),
            jnp.stack(cmp_s), jnp.stack(sel_p), jnp.stack(sel_s), jnp.stack(win_p), jnp.stack(win_s))
```

```python
import functools
import math

import numpy as np
import jax
import jax.numpy as jnp
from jax import lax
from jax.experimental import pallas as pl
from jax.experimental.pallas import tpu as pltpu

F32 = jnp.float32
BF16 = jnp.bfloat16

HEAD_DIM = 64
NSA_GROUPS = 4
NSA_REP = 4
CMP_BLOCK = 32
SEL_BLOCK = 64
N_SEL = 16
WINDOW = 512
N_BUCKETS = 32
MAX_DISTANCE = 128
LN_EPS = 1e-5
SEL_FORCE = 1e4
NEG = -1e30

LANES = 128
ROW_TILE = 256
TAIL_TILE = 512
SB_TILE = 256
NSA_TQ = 256
SB_PAGES_PER_STEP = 16
CMP_PAGES_PER_STEP = 16
CMP_DOT_ROWS = 256
SELECT_SEQS = 8
VMEM_LIMIT = 56 * 1024 * 1024

KV_W = 2 * NSA_GROUPS * HEAD_DIM
HALF_W = NSA_GROUPS * HEAD_DIM

_NT = (((1,), (1,)), ((), ()))


def _dot(a, b):
    return jnp.dot(a, b, preferred_element_type=F32)


def _dot_nt(a, b):
    return lax.dot_general(a, b, _NT, preferred_element_type=F32)


def _split2(x):
    hi = x.astype(BF16)
    lo = (x - hi.astype(F32)).astype(BF16)
    return hi, lo


def _split3(x):
    hi = x.astype(BF16)
    r = x - hi.astype(F32)
    mid = r.astype(BF16)
    lo = (r - mid.astype(F32)).astype(BF16)
    return hi, mid, lo


def _dot_exact(x, m):
    hi, mid, lo = _split3(x)
    return _dot(hi, m) + _dot(mid, m) + _dot(lo, m)


def _softplus(z):
    return jnp.maximum(z, 0.0) + jnp.log(1.0 + jnp.exp2(jnp.abs(z) * (-1.0 / math.log(2.0))))


def _layer_norm(y, g, b):
    mu = jnp.mean(y, axis=-1, keepdims=True)
    yc = y - mu
    var = jnp.mean(yc * yc, axis=-1, keepdims=True)
    return yc * lax.rsqrt(var + LN_EPS) * g + b


def _bf16_round(x):
    return x.astype(BF16).astype(F32)


def _params(*sem):
    return pltpu.CompilerParams(dimension_semantics=sem, vmem_limit_bytes=VMEM_LIMIT)


def _const_spec(a):
    nd = a.ndim
    return pl.BlockSpec(a.shape, lambda *_: (0,) * nd, pipeline_mode=pl.Buffered(1))


def _row_call(body, row_args, const_args, outs, name, t_outs=(), seq_len=None, row_tile=ROW_TILE):
    m = row_args[0].shape[0]
    tm = min(row_tile, m)
    assert m % tm == 0
    in_specs = [pl.BlockSpec((tm, a.shape[1]), lambda i: (i, 0)) for a in row_args]
    in_specs += [_const_spec(a) for a in const_args]
    out_specs = [pl.BlockSpec((tm, o.shape[1]), lambda i: (i, 0)) for o in outs]
    if t_outs:
        nt = seq_len // tm
        assert seq_len % tm == 0
        out_specs += [pl.BlockSpec((1, o.shape[1], tm), lambda i: (i // nt, 0, i % nt)) for o in t_outs]
    return pl.pallas_call(
        body, grid=(m // tm,), in_specs=in_specs, out_specs=out_specs, out_shape=list(outs) + list(t_outs),
        compiler_params=_params("parallel"), name=name)(*row_args, *const_args)


def _sb_proj_rows_body(x_ref, wq_ref, wkv_ref, q_ref, kv_ref):
    xb = x_ref[...].astype(BF16)
    q_ref[...] = (_dot(xb, wq_ref[...]) * HEAD_DIM ** -0.5).astype(BF16)
    kv_ref[...] = _dot(xb, wkv_ref[...])


def _sb_proj_rows(x, wq, wkv):
    m, d = x.shape
    outs = [jax.ShapeDtypeStruct((m, d), BF16), jax.ShapeDtypeStruct((m, 2 * d), F32)]
    return _row_call(_sb_proj_rows_body, [x], [wq, wkv], outs, "sb_proj_rows")


def _sb_proj_body(x_ref, wq_ref, wv_ref, wkvt_ref, q_ref, v_ref, kvt_ref, kt_ref):
    xb = x_ref[...].astype(BF16)
    q_ref[...] = (_dot(xb, wq_ref[...]) * HEAD_DIM ** -0.5).astype(BF16)
    v_ref[...] = _dot(xb, wv_ref[...]).astype(BF16)
    kvt = _dot_nt(wkvt_ref[...], xb)
    kvt_ref[0] = kvt
    kt_ref[0] = kvt[:kt_ref.shape[1]].astype(BF16)


def _sb_proj(x, wq, wv, wkvt, batch):
    m, d = x.shape
    t = m // batch
    outs = [jax.ShapeDtypeStruct((m, d), BF16), jax.ShapeDtypeStruct((m, d), BF16)]
    t_outs = [jax.ShapeDtypeStruct((batch, 2 * d, t), F32), jax.ShapeDtypeStruct((batch, d, t), BF16)]
    return _row_call(_sb_proj_body, [x], [wq, wv, wkvt], outs, "sb_proj", t_outs, t)


def _tail_body(x_ref, m_ref, wo_ref, g1_ref, b1_ref, wu_ref, wd_ref, g2_ref, b2_ref, o_ref, *, alpha):
    y = alpha * x_ref[...] + _dot(m_ref[...].astype(BF16), wo_ref[...])
    y = _layer_norm(y, g1_ref[...], b1_ref[...])
    h = jnp.maximum(_dot(y.astype(BF16), wu_ref[...]), 0.0)
    y = alpha * y + _dot((h * h).astype(BF16), wd_ref[...])
    o_ref[...] = _layer_norm(y, g2_ref[...], b2_ref[...])


def _tail(x, mix, wo, g1, b1, wu, wd, g2, b2, alpha):
    outs = [jax.ShapeDtypeStruct(x.shape, F32)]
    body = functools.partial(_tail_body, alpha=alpha)
    return _row_call(body, [x, mix], [wo, g1, b1, wu, wd, g2, b2], outs, "block_tail", row_tile=TAIL_TILE)[0]


def _rev_cumsum_matrix(n):
    j = np.arange(n)
    u = j[:, None] >= j[None, :]
    return jnp.asarray(np.concatenate([u, u], axis=0), BF16)


def _sb_local(z, u, mask):
    sp = _softplus(z)
    if mask is not None:
        sp = jnp.where(mask, sp, 0.0)
    cum = _dot(jnp.concatenate(_split2(sp), axis=1), u)
    return z - cum, cum[:, :1]


def _sb_weights(d, suf, mask):
    a = jnp.exp(d - suf)
    if mask is not None:
        a = jnp.where(mask, a, 0.0)
    return a.astype(BF16)


def _sb_prompt_body(q_ref, kt_ref, v_ref, u_ref, o_ref, acc_ref):
    tq = q_ref.shape[0]
    qi = pl.program_id(2)
    lane = lax.broadcasted_iota(jnp.int32, (tq, LANES), 1)
    lo_half = lane < HEAD_DIM
    q2 = q_ref[...]
    zero = jnp.zeros_like(q2)
    qs = (jnp.where(lo_half, q2, zero), jnp.where(lo_half, zero, q2))
    u = u_ref[...]
    row = lax.broadcasted_iota(jnp.int32, (tq, tq), 0)
    col = lax.broadcasted_iota(jnp.int32, (tq, tq), 1)
    strictly_before = col < row
    acc_ref[...] = jnp.zeros_like(acc_ref)

    def tiles(js, sufs, mask):
        local = []
        for j in js:
            start = pl.multiple_of(j * tq, tq)
            kt2 = kt_ref[0, :, pl.ds(start, tq)]
            v2 = v_ref[pl.ds(start, tq), :]
            local.append((v2, [_sb_local(_dot(qs[h], kt2), u, mask) for h in range(2)]))
        sufs = list(sufs)
        for v2, per_head in local:
            for h, (d, tot) in enumerate(per_head):
                acc_ref[h] += _dot(_sb_weights(d, sufs[h], mask), v2)
                sufs[h] = sufs[h] + tot
        return tuple(sufs)

    zero_suf = jnp.zeros((tq, 1), F32)
    sufs = tiles([qi], (zero_suf, zero_suf), strictly_before)
    sufs = lax.fori_loop(0, qi // 2, lambda jj, s: tiles([qi - 1 - 2 * jj, qi - 2 - 2 * jj], s, None), sufs)

    @pl.when(qi % 2 == 1)
    def _():
        tiles([0], sufs, None)

    o_ref[...] = jnp.where(lo_half, acc_ref[0], acc_ref[1]).astype(BF16)


def _sb_prompt(q, kt, v, batch):
    m, d = q.shape
    t = m // batch
    tq = min(SB_TILE, t)
    assert t % tq == 0 and d % LANES == 0
    nq = t // tq
    u = _rev_cumsum_matrix(tq)
    return pl.pallas_call(
        _sb_prompt_body, grid=(batch, d // LANES, nq),
        in_specs=[pl.BlockSpec((tq, LANES), lambda b, h, i: (b * nq + i, h)),
                  pl.BlockSpec((1, LANES, t), lambda b, h, i: (b, h, 0)),
                  pl.BlockSpec((t, LANES), lambda b, h, i: (b, h)),
                  pl.BlockSpec((2 * tq, tq), lambda b, h, i: (0, 0))],
        out_specs=pl.BlockSpec((tq, LANES), lambda b, h, i: (b * nq + i, h)),
        out_shape=jax.ShapeDtypeStruct((m, d), BF16),
        scratch_shapes=[pltpu.VMEM((2, tq, LANES), F32)],
        compiler_params=_params("parallel", "parallel", "arbitrary"), name="sb_prompt")(q, kt, v, u)


def _position_minor_pages(cache):
    n_layers, n_pool, page = cache.shape[:3]
    feat = cache.shape[3] * cache.shape[4] * cache.shape[5]
    return jnp.transpose(cache, (0, 1, 3, 4, 5, 2)).reshape(n_layers * n_pool, feat, page)


def _sb_sample_body(pt_ref, q_ref, u_ref, *rest, pages_per_step):
    pages = rest[:pages_per_step]
    o_ref, acc_ref, suf_ref = rest[pages_per_step:]
    s = pl.program_id(1)
    d = q_ref.shape[-1]
    heads = d // HEAD_DIM

    @pl.when(s == 0)
    def _():
        acc_ref[...] = jnp.zeros_like(acc_ref)
        suf_ref[...] = jnp.zeros_like(suf_ref)

    own = (lax.broadcasted_iota(jnp.int32, (heads, d), 1) // HEAD_DIM
           == lax.broadcasted_iota(jnp.int32, (heads, d), 0))
    qrow = jnp.broadcast_to(q_ref[0].astype(F32), (heads, d))
    qbd = jnp.where(own, qrow, 0.0).astype(BF16)
    u = u_ref[...]
    acc = acc_ref[...]
    suf = suf_ref[...]
    local = []
    for p in range(pages_per_step):
        kt = pages[p][0, :d, :].astype(BF16)
        local.append(_sb_local(_dot(qbd, kt), u, None))
    for p, (dl, tot) in enumerate(local):
        vt = pages[p][0, d:, :].astype(BF16)
        acc = acc + _dot_nt(_sb_weights(dl, suf, None), vt)
        suf = suf + tot
    acc_ref[...] = acc
    suf_ref[...] = suf

    @pl.when(s == pl.num_programs(1) - 1)
    def _():
        o_ref[0] = jnp.sum(jnp.where(own, acc, 0.0), axis=0, keepdims=True)


def _sb_sample(q, cache, layer, page_table):
    b, d = q.shape
    n_pool, page = cache.shape[1:3]
    assert cache.shape[3] * cache.shape[4] * cache.shape[5] == 2 * d
    n_pages = page_table.shape[1]
    pps = math.gcd(SB_PAGES_PER_STEP, n_pages)
    pages = _position_minor_pages(cache)
    base = layer * n_pool

    def page_spec(p):
        return pl.BlockSpec((1, 2 * d, page),
                            lambda i, s, pt: (base + pt[i, n_pages - 1 - (s * pps + p)], 0, 0))

    grid_spec = pltpu.PrefetchScalarGridSpec(
        num_scalar_prefetch=1, grid=(b, n_pages // pps),
        in_specs=[pl.BlockSpec((1, 1, d), lambda i, s, pt: (i, 0, 0)),
                  pl.BlockSpec((2 * page, page), lambda i, s, pt: (0, 0))]
                 + [page_spec(p) for p in range(pps)],
        out_specs=pl.BlockSpec((1, 1, d), lambda i, s, pt: (i, 0, 0)),
        scratch_shapes=[pltpu.VMEM((d // HEAD_DIM, d), F32), pltpu.VMEM((d // HEAD_DIM, 1), F32)])
    out = pl.pallas_call(
        functools.partial(_sb_sample_body, pages_per_step=pps), grid_spec=grid_spec,
        out_shape=jax.ShapeDtypeStruct((b, 1, d), F32),
        compiler_params=_params("parallel", "arbitrary"), name="sb_sample")(
            page_table, q.reshape(b, 1, d), _rev_cumsum_matrix(page), *([pages] * pps))
    return out.reshape(b, d)


def _nsa_proj_rows_body(x_ref, wq_ref, wkv_ref, wg_ref, q_ref, kv_ref, g_ref):
    xb = x_ref[...].astype(BF16)
    q_ref[...] = (_dot(xb, wq_ref[...]) * HEAD_DIM ** -0.5).astype(BF16)
    kv_ref[...] = _dot(xb, wkv_ref[...])
    g_ref[...] = jax.nn.sigmoid(_dot(xb, wg_ref[...]))


def _nsa_proj_rows(x, wq, wkv, wg):
    m, d = x.shape
    outs = [jax.ShapeDtypeStruct((m, d), BF16), jax.ShapeDtypeStruct((m, wkv.shape[1]), F32),
            jax.ShapeDtypeStruct((m, wg.shape[1]), F32)]
    return _row_call(_nsa_proj_rows_body, [x], [wq, wkv, wg], outs, "nsa_proj_rows")


def _nsa_proj_body(x_ref, wq_ref, wc_ref, wg_ref, wkvt_ref, q_ref, cmp_ref, g_ref,
                   cmpt_ref, selt_ref, wint_ref, seltb_ref, wintb_ref):
    xb = x_ref[...].astype(BF16)
    q_ref[...] = (_dot(xb, wq_ref[...]) * HEAD_DIM ** -0.5).astype(BF16)
    cmp_ref[...] = _dot(xb, wc_ref[...])
    g_ref[...] = jax.nn.sigmoid(_dot(xb, wg_ref[...]))
    kvt = _dot_nt(wkvt_ref[...], xb)
    cmpt_ref[0] = kvt[:KV_W]
    sel = kvt[KV_W:2 * KV_W]
    win = kvt[2 * KV_W:]
    selt_ref[0] = sel
    wint_ref[0] = win
    seltb_ref[0] = sel.astype(BF16)
    wintb_ref[0] = win.astype(BF16)


def _nsa_proj(x, wq, wc, wg, wkvt, batch):
    m, d = x.shape
    t = m // batch
    outs = [jax.ShapeDtypeStruct((m, d), BF16), jax.ShapeDtypeStruct((m, KV_W), F32),
            jax.ShapeDtypeStruct((m, wg.shape[1]), F32)]
    t_outs = [jax.ShapeDtypeStruct((batch, KV_W, t), F32)] * 3 + [jax.ShapeDtypeStruct((batch, KV_W, t), BF16)] * 2
    return _row_call(_nsa_proj_body, [x], [wq, wc, wg, wkvt], outs, "nsa_proj", t_outs, t)


CMP_L_CHUNK = 4


def _compress_body(x_ref, pos_ref, w_ref, o_ref, acc_ref):
    kc = pl.program_id(1)

    @pl.when(kc == 0)
    def _():
        acc_ref[...] = jnp.zeros_like(acc_ref)

    xb = (x_ref[...] + pos_ref[...]).astype(BF16)
    for l in range(CMP_L_CHUNK):
        for c in range(2):
            lo = l * KV_W + c * HALF_W
            acc_ref[:, c * HALF_W:(c + 1) * HALF_W] += _dot(xb[:, lo:lo + HALF_W], w_ref[l, c])

    @pl.when(kc == pl.num_programs(1) - 1)
    def _():
        o_ref[...] = acc_ref[...].astype(BF16)


def _compress(x2, pos_flat, w2):
    rows, width = x2.shape
    tm = min(512, rows)
    assert rows % tm == 0 and CMP_BLOCK % CMP_L_CHUNK == 0
    kw = CMP_L_CHUNK * KV_W
    return pl.pallas_call(
        _compress_body, grid=(rows // tm, width // kw),
        in_specs=[pl.BlockSpec((tm, kw), lambda i, k: (i, k)),
                  pl.BlockSpec((1, kw), lambda i, k: (0, k)),
                  pl.BlockSpec((CMP_L_CHUNK, 2, HALF_W, HALF_W), lambda i, k: (k, 0, 0, 0))],
        out_specs=pl.BlockSpec((tm, KV_W), lambda i, k: (i, 0)),
        out_shape=jax.ShapeDtypeStruct((rows, KV_W), BF16),
        scratch_shapes=[pltpu.VMEM((tm, KV_W), F32)],
        compiler_params=_params("parallel", "arbitrary"), name="nsa_compress")(x2, pos_flat, w2)


def _compress_paged_body(pt_ref, pos_ref, perm_ref, w_ref, *rest, pages_per_step, steps_per_dot):
    pages = rest[:pages_per_step]
    o_ref, x_ref = rest[pages_per_step:]
    s = pl.program_id(1)
    slot = s % steps_per_dot
    pos = pos_ref[...]
    group = perm_ref.shape[0] // pages[0].shape[2]
    slab = perm_ref.shape[0] // CMP_BLOCK
    r0 = pl.multiple_of(slot * (pages_per_step // group) * slab, slab)
    for pg in range(pages_per_step // group):
        a = jnp.concatenate([pages[pg * group + k][0] + pos for k in range(group)], axis=1).astype(BF16)
        xp = _dot_nt(perm_ref[...], a)
        for l in range(CMP_BLOCK):
            x_ref[l, pl.ds(r0 + pg * slab, slab), :] = xp[l * slab:(l + 1) * slab, :]

    @pl.when(slot == steps_per_dot - 1)
    def _():
        for c in range(2):
            xc = jnp.concatenate([x_ref[l, :, c * HALF_W:(c + 1) * HALF_W].astype(BF16)
                                  for l in range(CMP_BLOCK)], axis=1)
            o_ref[:, c * HALF_W:(c + 1) * HALF_W] = _dot(xc, w_ref[c]).astype(BF16)


def _compress_paged(cache, layer, page_table, pos_rows, w2):
    n_pool, page = cache.shape[1:3]
    bs, n_pages = page_table.shape
    rpp = page // CMP_BLOCK
    pages = _position_minor_pages(cache)
    base = layer * n_pool
    pps = math.gcd(CMP_PAGES_PER_STEP, n_pages)
    spd = math.gcd(max(1, CMP_DOT_ROWS // (pps * rpp)), n_pages // pps)
    rows = pps * spd * rpp
    n_steps = n_pages // pps
    group = 8 // rpp
    assert 8 % rpp == 0 and pps % group == 0
    pos_t = jnp.tile(pos_rows, (rpp, 1)).T
    l_, k_, n_ = np.meshgrid(np.arange(CMP_BLOCK), np.arange(group), np.arange(rpp), indexing="ij")
    perm = np.zeros((group * page, group * page), np.float32)
    perm[(l_ * group * rpp + k_ * rpp + n_).ravel(), (k_ * page + n_ * CMP_BLOCK + l_).ravel()] = 1.0

    def page_spec(p):
        return pl.BlockSpec((1, KV_W, page), lambda i, s, pt: (base + pt[i, s * pps + p], 0, 0))

    grid_spec = pltpu.PrefetchScalarGridSpec(
        num_scalar_prefetch=1, grid=(bs, n_steps),
        in_specs=[pl.BlockSpec((KV_W, page), lambda i, s, pt: (0, 0)),
                  pl.BlockSpec(perm.shape, lambda i, s, pt: (0, 0)),
                  pl.BlockSpec((2, CMP_BLOCK * HALF_W, HALF_W), lambda i, s, pt: (0, 0, 0))]
                 + [page_spec(p) for p in range(pps)],
        out_specs=pl.BlockSpec((rows, KV_W), lambda i, s, pt: (i * (n_steps // spd) + s // spd, 0)),
        scratch_shapes=[pltpu.VMEM((CMP_BLOCK, rows, KV_W), F32)])
    body = functools.partial(_compress_paged_body, pages_per_step=pps, steps_per_dot=spd)
    return pl.pallas_call(
        body, grid_spec=grid_spec, out_shape=jax.ShapeDtypeStruct((bs * n_pages * rpp, KV_W), BF16),
        compiler_params=_params("parallel", "arbitrary"), name="nsa_compress_paged")(
            page_table, pos_t, jnp.asarray(perm, BF16),
            jnp.swapaxes(w2, 0, 1).reshape(2, CMP_BLOCK * HALF_W, HALF_W), *([pages] * pps))


def _rel_bucket(dist):
    dist = jnp.maximum(dist, 0)
    max_exact = N_BUCKETS // 2
    dd = jnp.maximum(dist, 1).astype(F32)
    large = max_exact + (jnp.log(dd / max_exact) / math.log(MAX_DISTANCE / max_exact)
                         * (N_BUCKETS - max_exact)).astype(jnp.int32)
    large = jnp.minimum(large, N_BUCKETS - 1)
    return jnp.where(dist < max_exact, dist, large)


def _bias_of(rel_bias, dist):
    onehot = jax.nn.one_hot(_rel_bucket(jnp.asarray(dist, jnp.int32)), N_BUCKETS, dtype=F32)
    return jnp.einsum('...k,kh->h...', onehot, rel_bias.astype(F32), precision=lax.Precision.HIGHEST)


def _bucket_saturates_from(d0, d1):
    dd = np.arange(d0, d1 + 1).astype(np.float32)
    max_exact = N_BUCKETS // 2
    large = max_exact + (np.log(dd / np.float32(max_exact)) / np.float32(math.log(MAX_DISTANCE / max_exact))
                         * np.float32(N_BUCKETS - max_exact)).astype(np.int32)
    return bool(np.all(large >= N_BUCKETS - 1)) and d0 >= max_exact


def _nsa_prompt_body(q_ref, kc_ref, vc_ref, ks_ref, vs_ref, kw_ref, vw_ref, g_ref, near_ref, far_ref, edge_ref,
                     cb_ref, e_ref, pair_ref, o_ref, qh_ref, sel_ref, zs_ref, mx_ref, den_ref, acc_ref, out_ref,
                     *, n_near):
    tq = q_ref.shape[0]
    nc = kc_ref.shape[0]
    rep = NSA_REP
    n_heads = 2 * rep
    qi = pl.program_id(2)
    lane = lax.broadcasted_iota(jnp.int32, (tq, LANES), 1)
    rowl = lax.broadcasted_iota(jnp.int32, (tq, LANES), 0)
    lo_half = lane < HEAD_DIM
    half_mask = (lo_half, jnp.logical_not(lo_half))
    gates = g_ref[...]

    def gate(branch, i):
        k = branch * n_heads + i
        return gates[:, k:k + 1]

    def rows(r):
        return slice(r * tq, (r + 1) * tq)

    for i in range(n_heads):
        hf, r = divmod(i, rep)
        qf = q_ref[:, (i // 2) * LANES:(i // 2 + 1) * LANES].astype(F32)
        if i % 2 != hf:
            qf = pltpu.roll(qf, HEAD_DIM, 1)
        qh_ref[hf, rows(r), :] = jnp.where(half_mask[hf], qf, 0.0).astype(BF16)

    coln = lax.broadcasted_iota(jnp.int32, (tq, nc), 1)
    rown = lax.broadcasted_iota(jnp.int32, (tq, nc), 0)
    valid_c = qi * tq + rown - (coln * CMP_BLOCK + CMP_BLOCK - 1) >= 0
    m_rel = jnp.clip((qi + 1) * (tq // CMP_BLOCK) - 1 - coln[:1], 0, n_near)
    pick_col = jnp.where(lax.broadcasted_iota(jnp.int32, (LANES, nc), 0) == m_rel, 1.0, 0.0).astype(BF16)
    kc = kc_ref[...]
    vc = vc_ref[...]
    n_blocks = ks_ref.shape[2] // SEL_BLOCK
    nb8 = -(-n_blocks // 8) * 8
    blk_t = lax.broadcasted_iota(jnp.int32, (nb8, tq), 0)
    for hf in range(2):
        imp = jnp.zeros((tq, nc), F32)
        for r in range(rep):
            i = hf * rep + r
            cbias = _dot_exact(cb_ref[i], pick_col)
            z = jnp.where(valid_c, _dot_nt(qh_ref[hf, rows(r), :], kc) + cbias, NEG)
            e = jnp.where(valid_c, jnp.exp(z - jnp.max(z, axis=-1, keepdims=True)), 0.0)
            p = e / jnp.maximum(jnp.sum(e, axis=-1, keepdims=True), 1e-30)
            imp = imp + p
            out_ref[i] = gate(0, i) * _dot(p.astype(BF16), vc)
        imp2 = _dot_exact(imp, pair_ref[...])
        qpos = qi * tq + rowl
        cur = qpos // SEL_BLOCK
        forced = (lane == 0) | (lane == cur) | (lane == cur - 1)
        score = jnp.where(lane * SEL_BLOCK > qpos, NEG, jnp.where(forced, SEL_FORCE, imp2))
        st = score.T[:nb8]
        rank = jnp.zeros((nb8, tq), F32)
        for j in range(n_blocks):
            sj = st[j:j + 1, :]
            beats = (sj > st) | ((sj == st) & (blk_t > j))
            rank = rank + jnp.where(beats, 1.0, 0.0)
        sel_t = jnp.where(rank < N_SEL, 1.0, 0.0)
        sel_ref[hf] = jnp.concatenate([sel_t, jnp.zeros((LANES - nb8, tq), F32)], axis=0).T.astype(BF16)

    n_win_tiles = WINDOW // tq + 1

    def bias_of(i, off, window):
        if off < 2:
            return near_ref[i, off]
        if window and off == n_win_tiles - 1:
            return far_ref[i] + edge_ref[...]
        return far_ref[i]

    def reset():
        mx_ref[...] = jnp.full_like(mx_ref, NEG)
        den_ref[...] = jnp.zeros_like(den_ref)
        acc_ref[...] = jnp.zeros_like(acc_ref)

    def score_tile(hf, kt_ref, start, slot, off, window):
        kt = kt_ref[0, :, pl.ds(start, tq)]
        z = _dot(qh_ref[hf], kt)
        if not window:
            smask = (_dot(sel_ref[hf], e_ref[:, pl.ds(start, tq)]) - 1.0) * (-NEG)
        for r in range(rep):
            zr = z[rows(r)] + bias_of(hf * rep + r, off, window)
            if not window:
                zr = zr + smask
            zs_ref[r, :, pl.ds(slot, tq)] = zr
            mx_ref[r] = jnp.maximum(mx_ref[r], zr)

    def fix_max():
        for r in range(rep):
            mx_ref[r] = jnp.broadcast_to(jnp.max(mx_ref[r], axis=-1, keepdims=True), (tq, tq))

    def value_tile(vt_ref, start, slot):
        vt = vt_ref[0, :, pl.ds(start, tq)]
        ps = []
        for r in range(rep):
            p = jnp.exp(zs_ref[r, :, pl.ds(slot, tq)] - mx_ref[r])
            den_ref[r] += p
            ps.append(p.astype(BF16))
        acc_ref[...] += _dot_nt(jnp.concatenate(ps, axis=0), vt)

    def finish(hf, branch):
        for r in range(rep):
            i = hf * rep + r
            den = jnp.sum(den_ref[r], axis=-1, keepdims=True)
            out_ref[i] += (gate(branch, i) / den) * acc_ref[rows(r), :]

    for hf in range(2):
        def sel_score(j, off):
            start = pl.multiple_of(j * tq, tq)
            score_tile(hf, ks_ref, start, start, off, False)

        reset()
        sel_score(qi, 0)
        pl.when(qi >= 1)(lambda: sel_score(qi - 1, 1))

        def far_step(jj, carry):
            sel_score(qi - 2 - jj, 2)
            return carry

        lax.fori_loop(0, jnp.maximum(qi - 1, 0), far_step, 0)
        fix_max()

        def sel_value(j, carry):
            start = pl.multiple_of(j * tq, tq)
            value_tile(vs_ref, start, start)
            return carry

        lax.fori_loop(0, qi + 1, sel_value, 0)
        finish(hf, 1)

        def win_score(off):
            score_tile(hf, kw_ref, pl.multiple_of((qi - off) * tq, tq), off * tq, off, True)

        def win_value(off):
            value_tile(vw_ref, pl.multiple_of((qi - off) * tq, tq), off * tq)

        reset()
        win_score(0)
        for off in range(1, n_win_tiles):
            pl.when(qi >= off)(functools.partial(win_score, off))
        fix_max()
        win_value(0)
        for off in range(1, n_win_tiles):
            pl.when(qi >= off)(functools.partial(win_value, off))
        finish(hf, 2)

    for c in range(n_heads // 2):
        halves = []
        for i in (2 * c, 2 * c + 1):
            o = out_ref[i]
            if i % 2 != i // rep:
                o = pltpu.roll(o, HEAD_DIM, 1)
            halves.append(o)
        o_ref[:, c * LANES:(c + 1) * LANES] = jnp.where(lo_half, halves[0], halves[1]).astype(BF16)


def _nsa_prompt(q, kvc, selt, wint, gates, rel_bias, batch):
    m, d = q.shape
    t = m // batch
    tq = min(NSA_TQ, t)
    nq = t // tq
    nc = t // CMP_BLOCK
    nb = t // SEL_BLOCK
    rep = NSA_REP
    n_heads = 2 * rep
    assert t % tq == 0 and nb <= LANES and d == 2 * n_heads * HEAD_DIM and KV_W == 4 * LANES
    assert WINDOW % tq == 0 and WINDOW // tq >= 2 and tq % LANES == 0
    assert _bucket_saturates_from(tq + 1, t + WINDOW)
    n_win_tiles = WINDOW // tq + 1
    zs_width = max(t, n_win_tiles * tq)

    tl = np.arange(tq)
    dist0 = tl[:, None] - tl[None, :]
    near = jnp.stack([_bias_of(rel_bias, dist0) + jnp.where(dist0 >= 0, 0.0, NEG),
                      _bias_of(rel_bias, tq + dist0)], axis=1)
    far = _bias_of(rel_bias, np.full((1, tq), 2 * tq))
    edge = jnp.asarray(np.where((n_win_tiles - 1) * tq + dist0 <= WINDOW, 0.0, NEG), F32)
    shift = tq // CMP_BLOCK - 1
    n_near = shift + -(-(tq + CMP_BLOCK) // CMP_BLOCK) + 1
    assert tq % CMP_BLOCK == 0 and CMP_BLOCK * (n_near - shift) - CMP_BLOCK + 1 > tq
    assert n_near < LANES
    cb = _bias_of(rel_bias, CMP_BLOCK * (np.arange(LANES)[None, :] - shift) + tl[:, None] - (CMP_BLOCK - 1))
    expand = jnp.asarray(np.arange(LANES)[:, None] == (np.arange(t) // SEL_BLOCK)[None, :], BF16)
    pair = jnp.asarray((np.arange(nc) // (SEL_BLOCK // CMP_BLOCK))[:, None] == np.arange(LANES)[None, :], BF16)

    kvc_spec = lambda col0: pl.BlockSpec((nc, LANES), lambda b, gp, i: (b, col0 + gp))
    kvt_spec = lambda row0: pl.BlockSpec((1, LANES, t), lambda b, gp, i: (b, row0 + gp, 0))
    return pl.pallas_call(
        functools.partial(_nsa_prompt_body, n_near=n_near), grid=(batch, 2, nq),
        in_specs=[pl.BlockSpec((tq, n_heads * HEAD_DIM), lambda b, gp, i: (b * nq + i, gp)),
                  kvc_spec(0), kvc_spec(2), kvt_spec(0), kvt_spec(2), kvt_spec(0), kvt_spec(2),
                  pl.BlockSpec((tq, LANES), lambda b, gp, i: (b * nq + i, gp)),
                  pl.BlockSpec((n_heads, 2, tq, tq), lambda b, gp, i: (gp, 0, 0, 0)),
                  pl.BlockSpec((n_heads, 1, tq), lambda b, gp, i: (gp, 0, 0)),
                  pl.BlockSpec((tq, tq), lambda b, gp, i: (0, 0)),
                  pl.BlockSpec((n_heads, tq, LANES), lambda b, gp, i: (gp, 0, 0)),
                  pl.BlockSpec((LANES, t), lambda b, gp, i: (0, 0)),
                  pl.BlockSpec((nc, LANES), lambda b, gp, i: (0, 0))],
        out_specs=pl.BlockSpec((tq, n_heads * HEAD_DIM), lambda b, gp, i: (b * nq + i, gp)),
        out_shape=jax.ShapeDtypeStruct((m, d), BF16),
        scratch_shapes=[pltpu.VMEM((2, rep * tq, LANES), BF16), pltpu.VMEM((2, tq, LANES), BF16),
                        pltpu.VMEM((rep, tq, zs_width), F32), pltpu.VMEM((rep, tq, tq), F32),
                        pltpu.VMEM((rep, tq, tq), F32), pltpu.VMEM((rep * tq, LANES), F32),
                        pltpu.VMEM((n_heads, tq, LANES), F32)],
        compiler_params=_params("parallel", "parallel", "arbitrary"), name="nsa_prompt")(
            q, kvc, kvc, selt, selt, wint, wint, gates, near, far, edge, cb, expand, pair)


def _nsa_select_body(q_ref, kvc_ref, cb_ref, pair_ref, oc_ref, idx_ref, *, past, n_take):
    n_seq = q_ref.shape[0]
    n_cmp = kvc_ref.shape[0] // n_seq
    imps = []
    for s in range(n_seq):
        kvc = kvc_ref[s * n_cmp:(s + 1) * n_cmp, :]
        z = _dot_nt(q_ref[s], kvc[:, :HALF_W]) + cb_ref[...]
        e = jnp.exp(z - jnp.max(z, axis=-1, keepdims=True))
        p = e / jnp.maximum(jnp.sum(e, axis=-1, keepdims=True), 1e-30)
        oc = _dot(p.astype(BF16), kvc[:, HALF_W:])
        for g in range(NSA_GROUPS):
            oc_ref[s, g * NSA_REP:(g + 1) * NSA_REP, :] = oc[g * NSA_REP:(g + 1) * NSA_REP,
                                                             g * HEAD_DIM:(g + 1) * HEAD_DIM]
        imps += [jnp.sum(p[g * NSA_REP:(g + 1) * NSA_REP], axis=0, keepdims=True) for g in range(NSA_GROUPS)]
    imp2 = _dot_exact(jnp.concatenate(imps, axis=0), pair_ref[...])
    rows, width = imp2.shape
    lane = lax.broadcasted_iota(jnp.int32, (rows, width), 1)
    lanef = lane.astype(F32)
    cur = past // SEL_BLOCK
    forced = (lane == 0) | (lane == cur) | (lane == cur - 1)
    score = jnp.where(lane * SEL_BLOCK > past, NEG, jnp.where(forced, SEL_FORCE, imp2))
    out_lane = lax.broadcasted_iota(jnp.int32, (rows, LANES), 1)
    idx = jnp.zeros((rows, LANES), F32)
    for k in range(n_take):
        best = jnp.max(score, axis=-1, keepdims=True)
        pick = jnp.min(jnp.where(score == best, lanef, float(width)), axis=-1, keepdims=True)
        idx = jnp.where(out_lane == k, pick, idx)
        score = jnp.where(lanef == pick, 2.0 * NEG, score)
    idx = idx.astype(jnp.int32)
    for s in range(n_seq):
        idx_ref[s] = idx[s * NSA_GROUPS:(s + 1) * NSA_GROUPS]


def _nsa_select(q16, kvc, cb, past):
    bs, n_heads, _ = q16.shape
    n_cmp = kvc.shape[0] // bs
    n_blocks = past // SEL_BLOCK + 1
    width = -(-n_blocks // LANES) * LANES
    n_take = min(N_SEL, n_blocks)
    n_seq = math.gcd(SELECT_SEQS, bs)
    pair = jnp.asarray((np.arange(n_cmp) // (SEL_BLOCK // CMP_BLOCK))[:, None] == np.arange(width)[None, :], BF16)
    body = functools.partial(_nsa_select_body, past=past, n_take=n_take)
    return pl.pallas_call(
        body, grid=(bs // n_seq,),
        in_specs=[pl.BlockSpec((n_seq, n_heads, HALF_W), lambda b: (b, 0, 0)),
                  pl.BlockSpec((n_seq * n_cmp, KV_W), lambda b: (b, 0)),
                  pl.BlockSpec((n_heads, n_cmp), lambda b: (0, 0)),
                  pl.BlockSpec((n_cmp, width), lambda b: (0, 0))],
        out_specs=[pl.BlockSpec((n_seq, n_heads, HEAD_DIM), lambda b: (b, 0, 0)),
                   pl.BlockSpec((n_seq, NSA_GROUPS, LANES), lambda b: (b, 0, 0))],
        out_shape=[jax.ShapeDtypeStruct((bs, n_heads, HEAD_DIM), F32),
                   jax.ShapeDtypeStruct((bs, NSA_GROUPS, LANES), jnp.int32)],
        compiler_params=_params("parallel"), name="nsa_select")(q16, kvc, cb, pair), n_take


BIAS_TABLE = 2 * LANES


def _lookup_bias(tab, dist):
    n = dist.shape[1]
    onehot = (lax.broadcasted_iota(jnp.int32, (BIAS_TABLE, n), 0) == jnp.minimum(dist, BIAS_TABLE - 1))
    return _dot_exact(tab, jnp.where(onehot, 1.0, 0.0).astype(BF16))


def _nsa_attend_body(pt_ref, ix_ref, q_ref, seln_ref, winn_ref, wcol_ref, win_ref, oc_ref, g_ref, tab_ref, wtab_ref,
                     pick_ref, *rest, past, n_take):
    pages = rest[:n_take]
    o_ref, wout_ref = rest[n_take:]
    b = pl.program_id(0)
    g = pl.program_id(1)
    n_past_blocks = past // SEL_BLOCK
    page = pages[0].shape[2]
    q4 = q_ref[0, 0]
    q4f = q4.astype(F32)
    tab = tab_ref[0]
    gates = g_ref[0, 0]

    has_new = False
    lane = lax.broadcasted_iota(jnp.int32, (1, page), 1)
    kpos, in_block = [], []
    for i in range(n_take):
        v = ix_ref[(b * NSA_GROUPS + g) * n_take + i]
        kp = (v * SEL_BLOCK) // page * page + lane
        kpos.append(kp)
        in_block.append(jnp.where((kp // SEL_BLOCK == v) & (v < n_past_blocks), 1, 0))
        has_new = jnp.logical_or(has_new, v >= n_past_blocks)
    kpos = jnp.concatenate(kpos, axis=1)
    dist = past - kpos
    ok = (jnp.concatenate(in_block, axis=1) == 1) & (dist >= 0)
    kv = jnp.concatenate([pages[i][0] for i in range(n_take)], axis=1).astype(BF16)
    z = jnp.where(ok, _dot(q4, kv) + _lookup_bias(tab, jnp.maximum(dist, 0)), NEG)
    new_s = _bf16_round(seln_ref[0])
    z_new = jnp.sum(q4f * new_s, axis=-1, keepdims=True) + tab[:, :1]
    z_new = jnp.where(has_new, z_new, NEG)
    m = jnp.maximum(jnp.max(z, axis=-1, keepdims=True), z_new)
    e = jnp.exp(z - m)
    e_new = jnp.exp(z_new - m)
    den = jnp.maximum(jnp.sum(e, axis=-1, keepdims=True) + e_new, 1e-30)
    o_s = _dot_nt((e / den).astype(BF16), kv) + _bf16_round(e_new / den) * new_s

    w = win_ref[0]
    wb = w.astype(BF16)
    new_w = _bf16_round(winn_ref[0])
    n_win = w.shape[1]
    zw = _dot(q4, wb) + wtab_ref[0][:, :n_win]
    zw_new = jnp.sum(q4f * new_w, axis=-1, keepdims=True) + wtab_ref[0][:, n_win:n_win + 1]
    mw = jnp.maximum(jnp.max(zw, axis=-1, keepdims=True), zw_new)
    ew = jnp.exp(zw - mw)
    ew_new = jnp.exp(zw_new - mw)
    denw = jnp.maximum(jnp.sum(ew, axis=-1, keepdims=True) + ew_new, 1e-30)
    o_w = _dot_nt((ew / denw).astype(BF16), wb) + _bf16_round(ew_new / denw) * new_w

    o_sw = gates[:, 1:2] * o_s + gates[:, 2:3] * o_w
    o_ref[0, 0] = gates[:, 0:1] * oc_ref[0, 0] + _dot_exact(o_sw, pick_ref[0])

    @pl.when(g == 0)
    def _():
        cols = lax.broadcasted_iota(jnp.int32, w.shape, 1)
        wout_ref[0] = jnp.where(cols == n_win - 1, wcol_ref[0], pltpu.roll(w, n_win - 1, 1))


def _nsa_attend(q4, sel_new, win_new, state_win, layer, cache_sel, page_table, idx, n_take, oc, gates, rel_bias):
    bs = q4.shape[0]
    n_pool, page = cache_sel.shape[1:3]
    n_pages = page_table.shape[1]
    past = n_pages * page
    n_past_blocks = past // SEL_BLOCK
    n_win = state_win.shape[2]
    assert _bucket_saturates_from(BIAS_TABLE - 1, past + SEL_BLOCK) and page % SEL_BLOCK == 0
    pages = _position_minor_pages(cache_sel)
    base = layer * n_pool
    win_t = jnp.transpose(state_win, (0, 1, 3, 4, 5, 2)).reshape(state_win.shape[0] * bs, KV_W, n_win)
    tab = _bias_of(rel_bias, np.arange(BIAS_TABLE)).reshape(NSA_GROUPS, NSA_REP, BIAS_TABLE)
    wpad = -(-(n_win + 1) // LANES) * LANES
    wdist = np.maximum(n_win - np.arange(wpad), 0)
    wtab = _bias_of(rel_bias, wdist).reshape(NSA_GROUPS, NSA_REP, wpad)
    feats = np.arange(KV_W)[None, :, None]
    pick = jnp.asarray(feats == HALF_W + np.arange(NSA_GROUPS)[:, None, None] * HEAD_DIM
                       + np.arange(HEAD_DIM)[None, None, :], BF16)

    def page_spec(i):
        def index(b, g, pt, ix):
            v = jnp.minimum(ix[(b * NSA_GROUPS + g) * n_take + i], n_past_blocks - 1)
            return (base + pt[b, (v * SEL_BLOCK) // page], 0, 0)
        return pl.BlockSpec((1, KV_W, page), index)

    grid_spec = pltpu.PrefetchScalarGridSpec(
        num_scalar_prefetch=2, grid=(bs, NSA_GROUPS),
        in_specs=[pl.BlockSpec((1, 1, NSA_REP, KV_W), lambda b, g, pt, ix: (b, g, 0, 0)),
                  pl.BlockSpec((1, 1, KV_W), lambda b, g, pt, ix: (b, 0, 0)),
                  pl.BlockSpec((1, 1, KV_W), lambda b, g, pt, ix: (b, 0, 0)),
                  pl.BlockSpec((1, KV_W, 1), lambda b, g, pt, ix: (b, 0, 0)),
                  pl.BlockSpec((1, KV_W, n_win), lambda b, g, pt, ix: (layer * bs + b, 0, 0)),
                  pl.BlockSpec((1, 1, NSA_REP, HEAD_DIM), lambda b, g, pt, ix: (b, g, 0, 0)),
                  pl.BlockSpec((1, 1, NSA_REP, LANES), lambda b, g, pt, ix: (b, g, 0, 0)),
                  pl.BlockSpec((1, NSA_REP, BIAS_TABLE), lambda b, g, pt, ix: (g, 0, 0)),
                  pl.BlockSpec((1, NSA_REP, wpad), lambda b, g, pt, ix: (g, 0, 0)),
                  pl.BlockSpec((1, KV_W, HEAD_DIM), lambda b, g, pt, ix: (g, 0, 0))]
                 + [page_spec(i) for i in range(n_take)],
        out_specs=[pl.BlockSpec((1, 1, NSA_REP, HEAD_DIM), lambda b, g, pt, ix: (b, g, 0, 0)),
                   pl.BlockSpec((1, KV_W, n_win), lambda b, g, pt, ix: (b, 0, 0))])
    body = functools.partial(_nsa_attend_body, past=past, n_take=n_take)
    return pl.pallas_call(
        body, grid_spec=grid_spec,
        out_shape=[jax.ShapeDtypeStruct((bs, NSA_GROUPS, NSA_REP, HEAD_DIM), F32),
                   jax.ShapeDtypeStruct((bs, KV_W, n_win), F32)],
        compiler_params=_params("parallel", "arbitrary"), name="nsa_attend")(
            page_table, idx, q4, sel_new, win_new, win_new.reshape(bs, KV_W, 1), win_t, oc, gates, tab, wtab, pick,
            *([pages] * n_take))


def _compress_weights(cmp_pos, w_cmp):
    pos = jnp.broadcast_to(jnp.swapaxes(cmp_pos, 0, 1)[:, :, None, :], (CMP_BLOCK, 2, NSA_GROUPS, HEAD_DIM))
    eye = jnp.eye(NSA_GROUPS, dtype=w_cmp.dtype)
    w2 = jnp.einsum('clde,gh->lcgdhe', w_cmp, eye).reshape(CMP_BLOCK, 2, HALF_W, HALF_W)
    return pos.reshape(CMP_BLOCK, KV_W), w2.astype(BF16)


def _nsa_gate_weights(wg, paired):
    d = wg.shape[0]
    if paired:
        w = wg.reshape(d, 3, 2, 2 * NSA_REP).transpose(0, 2, 1, 3).reshape(d, 2, 3 * 2 * NSA_REP)
        w = jnp.pad(w, ((0, 0), (0, 0), (0, LANES - w.shape[2])))
        return w.reshape(d, 2 * LANES).astype(BF16)
    return jnp.pad(wg, ((0, 0), (0, LANES - wg.shape[1]))).astype(BF16)


def _rows_by_position(x_t, lead):
    batch, _, t = x_t.shape
    nd = len(lead)
    return jnp.transpose(x_t.reshape(batch, *lead, t), (0, nd + 1) + tuple(range(1, nd + 1)))


def kernel(x_prompt, x_sample, cache_sb_kv, cache_cmp_kv, cache_sel_kv, state_win_kv, page_table,
           w_in_sb, w_out_sb, w_in_nsa, w_out_nsa, cmp_pos, w_cmp, rel_bias, ln_g, ln_b, w_up, w_down):
    bp, t, d = x_prompt.shape
    bs, tn, _ = x_sample.shape
    assert tn == 1 and d == NSA_GROUPS * NSA_REP * HEAD_DIM
    depth = ln_g.shape[0]
    alpha = (2 * depth) ** 0.25
    past = page_table.shape[1] * cache_sb_kv.shape[2]
    n_heads = d // HEAD_DIM
    sb_lead = (2, n_heads, HEAD_DIM)
    nsa_lead = (2, NSA_GROUPS, HEAD_DIM)
    xp = x_prompt.reshape(bp * t, d)
    xs = x_sample.reshape(bs, d)
    sb_p, sb_s, cmp_p, cmp_s, sel_p, sel_s, win_p, win_s = [], [], [], [], [], [], [], []
    for i in range(depth):
        l = i // 2
        if i % 2 == 0:
            w = w_in_sb[l].astype(BF16)
            wq, wkv = w[:, :d], w[:, d:]
            q, v, kvt, kt = _sb_proj(xp, wq, w[:, 2 * d:], wkv.T, bp)
            mp = _sb_prompt(q, kt, v, bp)
            qs, kvs = _sb_proj_rows(xs, wq, wkv)
            ms = _sb_sample(qs, cache_sb_kv, l, page_table)
            sb_p.append(_rows_by_position(kvt, sb_lead))
            sb_s.append(kvs.reshape(bs, tn, *sb_lead))
            w_out = w_out_sb[l]
        else:
            w = w_in_nsa[l].astype(BF16)
            wq = w[:, :d]
            wkv = w[:, d:d + 3 * KV_W]
            wg = w_in_nsa[l][:, d + 3 * KV_W:]
            pos_rows, w2 = _compress_weights(cmp_pos[l], w_cmp[l])
            q, c_rows, gates, c_t, s_t, w_t, s_tb, w_tb = _nsa_proj(
                xp, wq, wkv[:, :KV_W], _nsa_gate_weights(wg, True), wkv.T, bp)
            kvc = _compress(c_rows.reshape(bp * t // CMP_BLOCK, CMP_BLOCK * KV_W), pos_rows.reshape(1, -1), w2)
            mp = _nsa_prompt(q, kvc, s_tb, w_tb, gates, rel_bias, bp)
            w_buf = min(WINDOW, t)
            cmp_p.append(_rows_by_position(c_t, nsa_lead))
            sel_p.append(_rows_by_position(s_t, nsa_lead))
            win_p.append(_rows_by_position(w_t[:, :, t - w_buf:], nsa_lead))
            qs, kv_s, gates_s = _nsa_proj_rows(xs, wq, wkv, _nsa_gate_weights(wg, False))
            c_s, s_s, w_s = kv_s[:, :KV_W], kv_s[:, KV_W:2 * KV_W], kv_s[:, 2 * KV_W:]
            kvc_s = _compress_paged(cache_cmp_kv, l, page_table, pos_rows, w2)
            qh = qs.reshape(bs, NSA_GROUPS, NSA_REP, 1, HEAD_DIM)
            eye = jnp.eye(NSA_GROUPS, dtype=qs.dtype)[None, :, None, :, None]
            q16 = (qh * eye).reshape(bs, n_heads, HALF_W)
            q4 = jnp.pad(q16, ((0, 0), (0, 0), (0, HALF_W))).reshape(bs, NSA_GROUPS, NSA_REP, KV_W)
            n_cmp = past // CMP_BLOCK
            cb_s = _bias_of(rel_bias, past - (np.arange(n_cmp) * CMP_BLOCK + CMP_BLOCK - 1))
            (oc, idx), n_take = _nsa_select(q16, kvc_s, cb_s, past)
            g3 = gates_s[:, :3 * n_heads].reshape(bs, 3, NSA_GROUPS, NSA_REP).transpose(0, 2, 3, 1)
            g3 = jnp.pad(g3, ((0, 0), (0, 0), (0, 0), (0, LANES - 3)))
            o4, wnew = _nsa_attend(q4, s_s.reshape(bs, 1, KV_W), w_s.reshape(bs, 1, KV_W), state_win_kv, l,
                                   cache_sel_kv, page_table, idx[:, :, :n_take].reshape(-1), n_take,
                                   oc.reshape(bs, NSA_GROUPS, NSA_REP, HEAD_DIM), g3, rel_bias)
            ms = o4.reshape(bs, d)
            cmp_s.append(c_s.reshape(bs, tn, *nsa_lead))
            sel_s.append(s_s.reshape(bs, tn, *nsa_lead))
            win_s.append(_rows_by_position(wnew, nsa_lead))
            w_out = w_out_nsa[l]
        tail_args = (w_out.astype(BF16), ln_g[i, 0][None], ln_b[i, 0][None], w_up[i].astype(BF16),
                     w_down[i].astype(BF16), ln_g[i, 1][None], ln_b[i, 1][None], alpha)
        xp = _tail(xp, mp, *tail_args)
        xs = _tail(xs, ms, *tail_args)
    return (xp.reshape(bp, t, d), xs.reshape(bs, tn, d), jnp.stack(sb_p), jnp.stack(sb_s), jnp.stack(cmp_p),
            jnp.stack(cmp_s), jnp.stack(sel_p), jnp.stack(sel_s), jnp.stack(win_p), jnp.stack(win_s))
```
